```python
import math
import jax, jax.numpy as jnp
from jax import lax
import numpy as np

D_MODEL = 2048
BATCH = 4
SEQ = 2048
DEPTH = 1
DEC_BATCH = 8
DEC_SEQ = 2048
PAST_LEN = 128

D_MIX = D_MODEL
ROPE_THETA = 10000.0
Q_BLOCK = 128
NORM_EPS = 1e-6

DIFF_HEADS = D_MODEL // 256
DIFF_QK_DIM = 64
DIFF_V_DIM = 2 * DIFF_QK_DIM
DIFF_WIDTH = DIFF_HEADS * DIFF_V_DIM

MLA_HEADS = D_MODEL // 256
MLA_NOPE = 128
MLA_ROPE = 64
MLA_V = 128
MLA_QK = MLA_NOPE + MLA_ROPE
Q_LORA = D_MODEL // 4
KV_LORA = D_MODEL // 8
MLA_WIDTH = MLA_HEADS * MLA_V

C_DQ = DIFF_HEADS * 2 * DIFF_QK_DIM
C_DK = DIFF_HEADS * 2 * DIFF_QK_DIM
C_DV = DIFF_WIDTH
C_MQ = Q_LORA
C_MKV = KV_LORA
C_MKR = MLA_ROPE
IN_COLS = C_DQ + C_DK + C_DV + C_MQ + C_MKV + C_MKR

N_EXPERTS = 16
CAPACITY_FACTOR = 2
D_FF_EXPERT = D_MODEL

kernel_name = "hybrid_diffattn_mla_ec_moe_encoder"


def rms_norm(x, g):
    xf = x.astype(jnp.float32)
    y = xf * lax.rsqrt(jnp.mean(xf * xf, axis=-1, keepdims=True) + NORM_EPS)
    return (y * g.astype(jnp.float32)).astype(x.dtype)


def rope(x):
    s, d = x.shape[1], x.shape[-1]
    half = d // 2
    inv = ROPE_THETA ** (-jnp.arange(half, dtype=jnp.float32) / half)
    ang = jnp.arange(s, dtype=jnp.float32)[:, None] * inv[None, :]
    cos = jnp.cos(ang)[None, :, None, :]
    sin = jnp.sin(ang)[None, :, None, :]
    xf = x.astype(jnp.float32)
    x1, x2 = xf[..., :half], xf[..., half:]
    return jnp.concatenate([x1 * cos - x2 * sin, x2 * cos + x1 * sin], axis=-1).astype(x.dtype)


def sweep_query_blocks(fn, *qs):
    b, s = qs[0].shape[:2]
    nb = s // Q_BLOCK
    blocks = tuple(jnp.moveaxis(q.reshape(b, nb, Q_BLOCK, *q.shape[2:]), 1, 0) for q in qs)
    out = lax.map(lambda a: fn(*a), blocks)
    return jnp.moveaxis(out, 0, 1).reshape(b, s, *out.shape[3:])


def differential_attention(q1, q2, k1, k2, v, lam):
    scale = DIFF_QK_DIM ** -0.5

    def block(q1b, q2b):
        s1 = jnp.einsum('bqhd,bkhd->bhqk', q1b, k1).astype(jnp.float32) * scale
        s2 = jnp.einsum('bqhd,bkhd->bhqk', q2b, k2).astype(jnp.float32) * scale
        a = jax.nn.softmax(s1, axis=-1) - lam * jax.nn.softmax(s2, axis=-1)
        return jnp.einsum('bhqk,bkhd->bqhd', a.astype(v.dtype), v)

    return sweep_query_blocks(block, q1, q2)


def softmax_attention(q, k, v):
    scale = q.shape[-1] ** -0.5

    def block(qb):
        s = jnp.einsum('bqhd,bkhd->bhqk', qb, k).astype(jnp.float32) * scale
        p = jax.nn.softmax(s, axis=-1)
        return jnp.einsum('bhqk,bkhd->bqhd', p.astype(v.dtype), v)

    return sweep_query_blocks(block, q)


def hybrid_mixer(h, w_in, lam_q1, lam_k1, lam_q2, lam_k2, g_diff_sub, g_q_lat, w_uq,
                 g_kv_lat, w_ukv, g_mla_out, w_out, lam_init):
    b, s, _ = h.shape
    proj = h @ w_in
    o1 = C_DQ
    o2 = o1 + C_DK
    o3 = o2 + C_DV
    o4 = o3 + C_MQ
    o5 = o4 + C_MKV
    qd = proj[..., :o1].reshape(b, s, DIFF_HEADS, 2, DIFF_QK_DIM)
    kd = proj[..., o1:o2].reshape(b, s, DIFF_HEADS, 2, DIFF_QK_DIM)
    vd = proj[..., o2:o3].reshape(b, s, DIFF_HEADS, DIFF_V_DIM)
    q1, q2 = rope(qd[..., 0, :]), rope(qd[..., 1, :])
    k1, k2 = rope(kd[..., 0, :]), rope(kd[..., 1, :])
    lam = (jnp.exp(jnp.sum(lam_q1.astype(jnp.float32) * lam_k1.astype(jnp.float32)))
           - jnp.exp(jnp.sum(lam_q2.astype(jnp.float32) * lam_k2.astype(jnp.float32)))
           + lam_init)
    oa = differential_attention(q1, q2, k1, k2, vd, lam)
    oa = (rms_norm(oa, g_diff_sub) * (1.0 - lam_init)).reshape(b, s, DIFF_WIDTH)
    cq = rms_norm(proj[..., o3:o4], g_q_lat) @ w_uq
    cq = cq.reshape(b, s, MLA_HEADS, MLA_QK)
    ckv = rms_norm(proj[..., o4:o5], g_kv_lat) @ w_ukv
    ckv = ckv.reshape(b, s, MLA_HEADS, MLA_NOPE + MLA_V)
    k_pe = rope(proj[..., o5:][:, :, None, :])
    qm = jnp.concatenate([cq[..., :MLA_NOPE], rope(cq[..., MLA_NOPE:])], axis=-1)
    km = jnp.concatenate([ckv[..., :MLA_NOPE],
                          jnp.broadcast_to(k_pe, (b, s, MLA_HEADS, MLA_ROPE))], axis=-1)
    vm = ckv[..., MLA_NOPE:]
    ob = softmax_attention(qm, km, vm).reshape(b, s, MLA_WIDTH)
    ob = rms_norm(ob, g_mla_out)
    return jnp.concatenate([oa, ob], axis=-1) @ w_out


def expert_choice_ffn(h, w_router, w_gate, w_up, w_down):
    b, s, d = h.shape
    n = b * s
    xt = h.reshape(n, d)
    aff = jax.nn.softmax((xt @ w_router).astype(jnp.float32), axis=-1)
    cap = CAPACITY_FACTOR * n // N_EXPERTS
    gate, idx = lax.top_k(aff.T, cap)
    xe = xt[idx]
    hg = jnp.einsum('ecd,edf->ecf', xe, w_gate)
    hu = jnp.einsum('ecd,edf->ecf', xe, w_up)
    ye = jnp.einsum('ecf,efd->ecd', jax.nn.silu(hg) * hu, w_down)
    ye = ye * gate[..., None].astype(ye.dtype)
    y = jnp.zeros_like(xt).at[idx.reshape(-1)].add(ye.reshape(-1, d))
    return y.reshape(b, s, d)


def encoder_layer(x, c, w_ada, b_ada, g_pre_attn, g_post_attn, g_pre_ffn, g_post_ffn,
                  w_in, lam_q1, lam_k1, lam_q2, lam_k2, g_diff_sub, g_q_lat, w_uq,
                  g_kv_lat, w_ukv, g_mla_out, w_out, w_router, w_gate, w_up, w_down,
                  lam_init):
    mod = (jax.nn.silu(c) @ w_ada + b_ada)[:, None, :]
    sh_a, sc_a, gt_a, sh_f, sc_f, gt_f = jnp.split(mod, 6, axis=-1)
    h = rms_norm(x, g_pre_attn) * (1.0 + sc_a) + sh_a
    m = hybrid_mixer(h, w_in, lam_q1, lam_k1, lam_q2, lam_k2, g_diff_sub, g_q_lat, w_uq,
                     g_kv_lat, w_ukv, g_mla_out, w_out, lam_init)
    x = x + gt_a * rms_norm(m, g_post_attn)
    h = rms_norm(x, g_pre_ffn) * (1.0 + sc_f) + sh_f
    f = expert_choice_ffn(h, w_router, w_gate, w_up, w_down)
    return x + gt_f * rms_norm(f, g_post_ffn)


def setup_inputs(seed: int = 0) -> dict:
    key = jax.random.key(seed)
    ks = iter(jax.random.split(key, 40))
    nrm = lambda shape, sc: jax.random.normal(next(ks), shape, jnp.float32) * sc
    gain = lambda shape: 1.0 + 0.02 * jax.random.normal(next(ks), shape, jnp.float32)
    L, D = DEPTH, D_MODEL
    return {
        "x_prompt": nrm((BATCH, SEQ, D), 1.0),
        "x_sample": nrm((DEC_BATCH, DEC_SEQ, D), 1.0),
        "c_prompt": nrm((BATCH, D), 1.0),
        "c_sample": nrm((DEC_BATCH, D), 1.0),
        "w_ada": nrm((L, D, 6 * D), 0.5 * D ** -0.5),
        "b_ada": nrm((L, 6 * D), 0.02),
        "g_pre_attn": gain((L, D)),
        "g_post_attn": gain((L, D)),
        "g_pre_ffn": gain((L, D)),
        "g_post_ffn": gain((L, D)),
        "w_in": nrm((L, D, IN_COLS), D ** -0.5),
        "lam_q1": nrm((L, DIFF_QK_DIM), 0.1),
        "lam_k1": nrm((L, DIFF_QK_DIM), 0.1),
        "lam_q2": nrm((L, DIFF_QK_DIM), 0.1),
        "lam_k2": nrm((L, DIFF_QK_DIM), 0.1),
        "g_diff_sub": gain((L, DIFF_V_DIM)),
        "g_q_lat": gain((L, Q_LORA)),
        "w_uq": nrm((L, Q_LORA, MLA_HEADS * MLA_QK), Q_LORA ** -0.5),
        "g_kv_lat": gain((L, KV_LORA)),
        "w_ukv": nrm((L, KV_LORA, MLA_HEADS * (MLA_NOPE + MLA_V)), KV_LORA ** -0.5),
        "g_mla_out": gain((L, MLA_WIDTH)),
        "w_out": nrm((L, D_MIX, D), D_MIX ** -0.5),
        "w_router": nrm((L, D, N_EXPERTS), D ** -0.5),
        "w_gate": nrm((L, N_EXPERTS, D, D_FF_EXPERT), D ** -0.5),
        "w_up": nrm((L, N_EXPERTS, D, D_FF_EXPERT), D ** -0.5),
        "w_down": nrm((L, N_EXPERTS, D_FF_EXPERT, D), D_FF_EXPERT ** -0.5),
    }


def reference(x_prompt, x_sample, c_prompt, c_sample, w_ada, b_ada, g_pre_attn, g_post_attn,
              g_pre_ffn, g_post_ffn, w_in, lam_q1, lam_k1, lam_q2, lam_k2, g_diff_sub,
              g_q_lat, w_uq, g_kv_lat, w_ukv, g_mla_out, w_out, w_router, w_gate, w_up,
              w_down):
    y_prompt, y_sample = x_prompt, x_sample
    for l in range(DEPTH):
        lam_init = 0.8 - 0.6 * math.exp(-0.3 * l)
        lp = (w_ada[l], b_ada[l], g_pre_attn[l], g_post_attn[l], g_pre_ffn[l], g_post_ffn[l],
              w_in[l], lam_q1[l], lam_k1[l], lam_q2[l], lam_k2[l], g_diff_sub[l], g_q_lat[l],
              w_uq[l], g_kv_lat[l], w_ukv[l], g_mla_out[l], w_out[l], w_router[l], w_gate[l],
              w_up[l], w_down[l])
        y_prompt = encoder_layer(y_prompt, c_prompt, *lp, lam_init)
        y_sample = encoder_layer(y_sample, c_sample, *lp, lam_init)
    return (y_prompt, y_sample)
```

```python
import functools
import math

import jax
import jax.numpy as jnp
from jax import lax
from jax.experimental import pallas as pl
from jax.experimental.pallas import tpu as pltpu

D_MODEL = 2048
DEPTH = 1
ROPE_THETA = 10000.0
NORM_EPS = 1e-6

HEADS = D_MODEL // 256
DIFF_QK = 64
HEAD_W = 128
MLA_ROPE = 64
MLA_QK = HEAD_W + MLA_ROPE
Q_LORA = D_MODEL // 4
KV_LORA = D_MODEL // 8
GROUP_W = HEADS * HEAD_W
N_EXPERTS = 16
CAPACITY_FACTOR = 2
D_FF = D_MODEL

LANES = 128
ROUTE_BLOCKS = 128
VMEM_LIMIT = 56 * 1024 * 1024

_O_QD = 0
_O_KD = _O_QD + GROUP_W
_O_VD = _O_KD + GROUP_W
_O_MQ = _O_VD + GROUP_W
_O_MKV = _O_MQ + Q_LORA
_O_KR = _O_MKV + KV_LORA
_W1_COLS = _O_KR + LANES

_f32 = jnp.float32
_bf16 = jnp.bfloat16


def _dot(a, b):
    return jnp.dot(a, b, preferred_element_type=_f32)


def _dot_nt(a, b):
    return lax.dot_general(a, b, (((1,), (1,)), ((), ())), preferred_element_type=_f32)


def _split(x):
    hi = x.astype(_bf16)
    lo = (x - hi.astype(_f32)).astype(_bf16)
    return hi, lo


def _dot3(a, b):
    ah, al = _split(a)
    bh, bl = _split(b)
    return _dot(ah, bh) + _dot(al, bh) + _dot(ah, bl)


def _rms(x, g):
    return x * lax.rsqrt(jnp.mean(x * x, axis=-1, keepdims=True) + NORM_EPS) * g


def _rope(x, cos, sin):
    return x * cos + pltpu.roll(x, 64, 1) * sin


def _params(sem, vmem=VMEM_LIMIT):
    return pltpu.CompilerParams(dimension_semantics=sem, vmem_limit_bytes=vmem)


def _const_spec(shape):
    nd = len(shape)
    return pl.BlockSpec(shape, lambda *_: (0,) * nd, pipeline_mode=pl.Buffered(1))


def _ada_kernel(c_ref, w_ref, b_ref, o_ref):
    c = c_ref[...]
    s = c * (1.0 / (1.0 + jnp.exp(-c)))
    o_ref[...] = _dot3(s, w_ref[...]) + b_ref[...]


def _ada(c, w, b):
    rows, d = c.shape
    cols = w.shape[1]
    tn = 1024
    return pl.pallas_call(
        _ada_kernel,
        grid=(cols // tn,),
        in_specs=[pl.BlockSpec((rows, d), lambda j: (0, 0)),
                  pl.BlockSpec((d, tn), lambda j: (0, j)),
                  pl.BlockSpec((1, tn), lambda j: (0, j))],
        out_specs=pl.BlockSpec((rows, tn), lambda j: (0, j)),
        out_shape=jax.ShapeDtypeStruct((rows, cols), _f32),
        compiler_params=_params(("arbitrary",)),
        name="ada",
    )(c, w, b)


def _proj_kernel(x_ref, sc_ref, sh_ref, g_ref, w1_ref, gq_ref, gkv_ref, wqn_ref, wqr_ref, wkv_ref,
                 cos_ref, sin_ref, qd_ref, kd_ref, vd_ref, qn_ref, qr_ref, knv_ref, kpe_ref):
    x = x_ref[0]
    h = _rms(x, g_ref[...]) * (1.0 + sc_ref[0]) + sh_ref[0]
    hb = h.astype(_bf16)
    cos = cos_ref[...]
    sin = sin_ref[...]

    q = _dot(hb, w1_ref[:, _O_QD:_O_QD + GROUP_W])
    for hd in range(HEADS):
        sl = slice(hd * LANES, (hd + 1) * LANES)
        qd_ref[0, :, sl] = (_rope(q[:, sl], cos, sin) * (DIFF_QK ** -0.5)).astype(_bf16)
    k = _dot(hb, w1_ref[:, _O_KD:_O_KD + GROUP_W])
    for hd in range(HEADS):
        sl = slice(hd * LANES, (hd + 1) * LANES)
        kd_ref[0, :, sl] = _rope(k[:, sl], cos, sin).astype(_bf16)
    vd_ref[0] = _dot(hb, w1_ref[:, _O_VD:_O_VD + GROUP_W]).astype(_bf16)

    ql = _rms(_dot(hb, w1_ref[:, _O_MQ:_O_MQ + Q_LORA]), gq_ref[...]).astype(_bf16)
    mscale = MLA_QK ** -0.5
    qn_ref[0] = (_dot(ql, wqn_ref[...]) * mscale).astype(_bf16)
    qr = _dot(ql, wqr_ref[...])
    for hd in range(HEADS):
        sl = slice(hd * LANES, (hd + 1) * LANES)
        qr_ref[0, :, sl] = (_rope(qr[:, sl], cos, sin) * mscale).astype(_bf16)

    kvl = _rms(_dot(hb, w1_ref[:, _O_MKV:_O_MKV + KV_LORA]), gkv_ref[...]).astype(_bf16)
    knv_ref[0] = _dot(kvl, wkv_ref[...]).astype(_bf16)
    kpe_ref[0] = _rope(_dot(hb, w1_ref[:, _O_KR:_O_KR + LANES]), cos, sin).astype(_bf16)


def _proj(x, sc, sh, g, w1, gq, gkv, wqn, wqr, wkv, cos, sin):
    b, s, d = x.shape
    tm = min(256, s)
    row = lambda width: pl.BlockSpec((1, tm, width), lambda bi, i: (bi, i, 0))
    mod = pl.BlockSpec((1, 1, d), lambda bi, i: (bi, 0, 0))
    tab = pl.BlockSpec((tm, LANES), lambda bi, i: (i, 0))
    widths = (GROUP_W, GROUP_W, GROUP_W, GROUP_W, GROUP_W, 2 * GROUP_W, LANES)
    return pl.pallas_call(
        _proj_kernel,
        grid=(b, s // tm),
        in_specs=[row(d), mod, mod, _const_spec(g.shape), _const_spec(w1.shape), _const_spec(gq.shape),
                  _const_spec(gkv.shape), _const_spec(wqn.shape), _const_spec(wqr.shape),
                  _const_spec(wkv.shape), tab, tab],
        out_specs=[row(w) for w in widths],
        out_shape=[jax.ShapeDtypeStruct((b, s, w), _bf16) for w in widths],
        compiler_params=_params(("arbitrary", "arbitrary")),
        name="proj",
    )(x, sc, sh, g, w1, gq, gkv, wqn, wqr, wkv, cos, sin)


def _softmax_pv(s, v):
    m = jnp.max(s, axis=-1, keepdims=True)
    p = jnp.exp(s - m)
    l = jnp.sum(p, axis=-1, keepdims=True)
    return _dot(p.astype(_bf16), v) / l


def _dattn_kernel(lq1_ref, lk1_ref, lq2_ref, lk2_ref, g_ref, q_ref, k_ref, v_ref, o_ref, *, lam_init):
    q = q_ref[0]
    k = k_ref[0]
    v = v_ref[0]
    lane = lax.broadcasted_iota(jnp.int32, q.shape, 1)
    first = (lane & 63) < 32
    zero = jnp.zeros_like(q)
    o1 = _softmax_pv(_dot_nt(jnp.where(first, q, zero), k), v)
    o2 = _softmax_pv(_dot_nt(jnp.where(first, zero, q), k), v)
    lam = (jnp.exp(jnp.sum(lq1_ref[...] * lk1_ref[...], axis=-1, keepdims=True))
           - jnp.exp(jnp.sum(lq2_ref[...] * lk2_ref[...], axis=-1, keepdims=True)) + lam_init)
    o = o1 - lam * o2
    o_ref[0] = (_rms(o, g_ref[...]) * (1.0 - lam_init)).astype(_bf16)


def _dattn(qd, kd, vd, lq1, lk1, lq2, lk2, g, lam_init):
    b, s, _ = qd.shape
    tq = min(256, s)
    small = lambda a: pl.BlockSpec(a.shape, lambda bi, h, i: (0, 0))
    return pl.pallas_call(
        functools.partial(_dattn_kernel, lam_init=lam_init),
        grid=(b, HEADS, s // tq),
        in_specs=[small(lq1), small(lk1), small(lq2), small(lk2), small(g),
                  pl.BlockSpec((1, tq, LANES), lambda bi, h, i: (bi, i, h)),
                  pl.BlockSpec((1, s, LANES), lambda bi, h, i: (bi, 0, h)),
                  pl.BlockSpec((1, s, LANES), lambda bi, h, i: (bi, 0, h))],
        out_specs=pl.BlockSpec((1, tq, LANES), lambda bi, h, i: (bi, i, h)),
        out_shape=jax.ShapeDtypeStruct((b, s, GROUP_W), _bf16),
        compiler_params=_params(("arbitrary",) * 3),
        name="dattn",
    )(lq1, lk1, lq2, lk2, g, qd, kd, vd)


def _mattn_kernel(qn_ref, qr_ref, kn_ref, v_ref, kpe_ref, o_ref):
    q = jnp.concatenate([qn_ref[0], qr_ref[0]], axis=-1)
    k = jnp.concatenate([kn_ref[0], kpe_ref[0]], axis=-1)
    o_ref[0] = _softmax_pv(_dot_nt(q, k), v_ref[0]).astype(_bf16)


def _mattn(qn, qr, knv, kpe):
    b, s, _ = qn.shape
    tq = min(256, s)
    return pl.pallas_call(
        _mattn_kernel,
        grid=(b, HEADS, s // tq),
        in_specs=[pl.BlockSpec((1, tq, LANES), lambda bi, h, i: (bi, i, h)),
                  pl.BlockSpec((1, tq, LANES), lambda bi, h, i: (bi, i, h)),
                  pl.BlockSpec((1, s, LANES), lambda bi, h, i: (bi, 0, 2 * h)),
                  pl.BlockSpec((1, s, LANES), lambda bi, h, i: (bi, 0, 2 * h + 1)),
                  pl.BlockSpec((1, s, LANES), lambda bi, h, i: (bi, 0, 0))],
        out_specs=pl.BlockSpec((1, tq, LANES), lambda bi, h, i: (bi, i, h)),
        out_shape=jax.ShapeDtypeStruct((b, s, GROUP_W), _bf16),
        compiler_params=_params(("arbitrary",) * 3),
        name="mattn",
    )(qn, qr, knv, knv, kpe)


def _post_kernel(x_ref, oa_ref, ob_ref, gt_ref, sc_ref, sh_ref, gmla_ref, gpost_ref, gpre_ref,
                 woa_ref, wob_ref, wrh_ref, wrl_ref, x1_ref, h2_ref, aff_ref):
    obn = _rms(ob_ref[0].astype(_f32), gmla_ref[...]).astype(_bf16)
    m = _dot(oa_ref[0], woa_ref[...]) + _dot(obn, wob_ref[...])
    x1 = x_ref[0] + gt_ref[0] * _rms(m, gpost_ref[...])
    x1_ref[0] = x1
    h2 = _rms(x1, gpre_ref[...]) * (1.0 + sc_ref[0]) + sh_ref[0]
    h2_ref[0] = h2
    hh, hl = _split(h2)
    logits = _dot(hh, wrh_ref[...]) + _dot(hl, wrh_ref[...]) + _dot(hh, wrl_ref[...])
    lane = lax.broadcasted_iota(jnp.int32, logits.shape, 1)
    logits = jnp.where(lane < N_EXPERTS, logits, -jnp.inf)
    p = jnp.exp(logits - jnp.max(logits, axis=-1, keepdims=True))
    aff = p / jnp.sum(p, axis=-1, keepdims=True)
    aff_ref[...] = aff.T[0:N_EXPERTS, :]


def _post(x, oa, ob, gt, sc, sh, gmla, gpost, gpre, woa, wob, wrh, wrl):
    b, s, d = x.shape
    tm = min(256, s)
    nt = s // tm
    row = lambda width: pl.BlockSpec((1, tm, width), lambda bi, i: (bi, i, 0))
    mod = pl.BlockSpec((1, 1, d), lambda bi, i: (bi, 0, 0))
    return pl.pallas_call(
        _post_kernel,
        grid=(b, nt),
        in_specs=[row(d), row(GROUP_W), row(GROUP_W), mod, mod, mod, _const_spec(gmla.shape),
                  _const_spec(gpost.shape), _const_spec(gpre.shape), _const_spec(woa.shape),
                  _const_spec(wob.shape), _const_spec(wrh.shape), _const_spec(wrl.shape)],
        out_specs=[row(d), row(d), pl.BlockSpec((N_EXPERTS, tm), lambda bi, i: (0, bi * nt + i))],
        out_shape=[jax.ShapeDtypeStruct((b, s, d), _f32), jax.ShapeDtypeStruct((b, s, d), _f32),
                   jax.ShapeDtypeStruct((N_EXPERTS, b * s), _f32)],
        compiler_params=_params(("arbitrary", "arbitrary")),
        name="post",
    )(x, oa, ob, gt, sc, sh, gmla, gpost, gpre, woa, wob, wrh, wrl)


def _route_kernel(aff_ref, idx_ref, gate_ref, selpos_ref, bex_ref, gt_scr, eq_scr, need_scr, *, cap, n_tok):
    nb = ROUTE_BLOCKS
    bits = pltpu.bitcast(aff_ref[...], jnp.int32)

    def count(mask):
        c = jnp.sum(jnp.where(mask, 1.0, 0.0), axis=2, keepdims=True)
        return jnp.sum(c, axis=1, keepdims=True)

    def search(i, thr):
        cand = thr | jnp.left_shift(jnp.int32(1), 30 - i)
        return jnp.where(count(bits >= cand) >= cap, cand, thr)

    thr = lax.fori_loop(0, 31, search, jnp.zeros((N_EXPERTS, 1, 1), jnp.int32))
    gt = bits > thr
    gt_scr[...] = jnp.where(gt, 1.0, 0.0)
    eq_scr[...] = jnp.where(bits == thr, 1.0, 0.0)
    need_scr[...] = jnp.broadcast_to(cap - count(gt), need_scr.shape)

    r_io = lax.broadcasted_iota(jnp.int32, (nb, LANES), 0)
    c_io = lax.broadcasted_iota(jnp.int32, (nb, LANES), 1)
    triu = jnp.where(r_io <= c_io, 1.0, 0.0).astype(_bf16)
    tril = jnp.where(c_io <= r_io, 1.0, 0.0).astype(_bf16)
    reps = cap // LANES
    slot = lax.broadcasted_iota(jnp.int32, (nb, cap), 1).astype(_f32)
    row = lax.broadcasted_iota(jnp.int32, (nb, cap), 0).astype(_f32)

    def prefix(m):
        loc = _dot(m.astype(_bf16), triu)
        tot = jnp.broadcast_to(loc[:, LANES - 1:LANES], loc.shape)
        return loc, tot, _dot(tril, tot.astype(_bf16))

    def per_expert(e, carry):
        gte = gt_scr[e]
        eqe = eq_scr[e]
        need = need_scr[e][0:1, :]
        eloc, etot, ebinc = prefix(eqe)
        eq_before = ebinc - etot + eloc - eqe
        sel = jnp.maximum(gte, eqe * jnp.where(eq_before < need, 1.0, 0.0))
        loc, tot, binc = prefix(sel)
        bexc = binc - tot
        pos = bexc + loc - sel
        selpos_ref[e] = jnp.where(sel > 0.0, pos, -1.0).astype(jnp.int32)
        bex_ref[e] = bexc.T[0:8, :]

        binc_t = jnp.tile(binc, (1, reps))
        bexc_t = jnp.tile(bexc, (1, reps))
        blk = jnp.sum(jnp.where(binc_t <= slot, 1.0, 0.0), axis=0, keepdims=True)
        hit = row == blk
        onehot = jnp.where(hit, 1.0, 0.0).astype(_bf16)
        rank = slot[0:1, :] - jnp.sum(jnp.where(hit, bexc_t, 0.0), axis=0, keepdims=True)
        loc_rows = _dot(loc.T.astype(_bf16), onehot)
        within = jnp.sum(jnp.where(loc_rows <= rank, 1.0, 0.0), axis=0, keepdims=True)
        token = jnp.minimum(blk * LANES + within, n_tok - 1.0)
        idx_ref[e] = jnp.broadcast_to(token, (8, cap)).astype(jnp.int32)

        at = aff_ref[e].T
        a0 = at.astype(_bf16)
        r1 = at - a0.astype(_f32)
        a1 = r1.astype(_bf16)
        a2 = (r1 - a1.astype(_f32)).astype(_bf16)
        aff_rows = _dot(a0, onehot) + _dot(a1, onehot) + _dot(a2, onehot)
        gate = jnp.sum(jnp.where(row == within, aff_rows, 0.0), axis=0, keepdims=True)
        gate_ref[e] = jnp.broadcast_to(gate, (8, cap))
        return carry

    lax.fori_loop(0, N_EXPERTS, per_expert, 0)


def _route(aff3, cap, n_tok):
    nb = ROUTE_BLOCKS
    full = lambda shape: pl.BlockSpec(shape, lambda: (0,) * len(shape))
    return pl.pallas_call(
        functools.partial(_route_kernel, cap=cap, n_tok=n_tok),
        in_specs=[full((N_EXPERTS, nb, LANES))],
        out_specs=[full((N_EXPERTS, 8, cap)), full((N_EXPERTS, 8, cap)), full((N_EXPERTS, nb, LANES)),
                   full((N_EXPERTS, 8, LANES))],
        out_shape=[jax.ShapeDtypeStruct((N_EXPERTS, 8, cap), jnp.int32),
                   jax.ShapeDtypeStruct((N_EXPERTS, 8, cap), _f32),
                   jax.ShapeDtypeStruct((N_EXPERTS, nb, LANES), jnp.int32),
                   jax.ShapeDtypeStruct((N_EXPERTS, 8, LANES), _f32)],
        scratch_shapes=[pltpu.VMEM((N_EXPERTS, nb, LANES), _f32), pltpu.VMEM((N_EXPERTS, nb, LANES), _f32),
                        pltpu.VMEM((N_EXPERTS, 8, LANES), _f32)],
        compiler_params=pltpu.CompilerParams(vmem_limit_bytes=VMEM_LIMIT),
        name="route",
    )(aff3)


def _ffn_kernel(idx_ref, gate_ref, h_hbm, wg_ref, wu_ref, wd_ref, o_ref, rows_scr, xe_scr, acc_scr, sem,
                *, tc, gc):
    f = pl.program_id(2)

    @pl.when(f == 0)
    def _():
        for ch in range(tc // gc):
            def issue(r, carry):
                t = idx_ref[0, 0, ch * gc + r]
                pltpu.make_async_copy(h_hbm.at[pl.ds(t, 1), :], rows_scr.at[pl.ds(r, 1), :], sem).start()
                return carry

            lax.fori_loop(0, gc, issue, 0)
            pltpu.make_async_copy(h_hbm.at[pl.ds(0, gc), :], rows_scr, sem).wait()
            xe_scr[ch * gc:(ch + 1) * gc, :] = rows_scr[...].astype(_bf16)
        acc_scr[...] = jnp.zeros_like(acc_scr)

    xe = xe_scr[...]
    hg = _dot(xe, wg_ref[...].astype(_bf16))
    hu = _dot(xe, wu_ref[...].astype(_bf16))
    a = (hg * (1.0 / (1.0 + jnp.exp(-hg))) * hu).astype(_bf16)
    acc_scr[...] += _dot(a, wd_ref[...].astype(_bf16))

    @pl.when(f == pl.num_programs(2) - 1)
    def _():
        o_ref[0] = (acc_scr[...] * gate_ref[0]).astype(_bf16)


def _ffn(idx3, gate3, h2, w_gate, w_up, w_down, cap):
    n, d = h2.shape
    tc = min(1024, cap)
    gc = min(256, tc)
    tf = min(256, D_FF)
    return pl.pallas_call(
        functools.partial(_ffn_kernel, tc=tc, gc=gc),
        grid=(N_EXPERTS, cap // tc, D_FF // tf),
        in_specs=[pl.BlockSpec((1, 1, tc), lambda e, i, f: (e * (cap // tc) + i, 0, 0), memory_space=pltpu.SMEM),
                  pl.BlockSpec((1, tc, 1), lambda e, i, f: (e, i, 0)),
                  pl.BlockSpec(memory_space=pl.ANY),
                  pl.BlockSpec((None, d, tf), lambda e, i, f: (e, 0, f)),
                  pl.BlockSpec((None, d, tf), lambda e, i, f: (e, 0, f)),
                  pl.BlockSpec((None, tf, d), lambda e, i, f: (e, f, 0))],
        out_specs=pl.BlockSpec((1, tc, d), lambda e, i, f: (e, i, 0)),
        out_shape=jax.ShapeDtypeStruct((N_EXPERTS, cap, d), _bf16),
        scratch_shapes=[pltpu.VMEM((gc, d), _f32), pltpu.VMEM((tc, d), _bf16), pltpu.VMEM((tc, d), _f32),
                        pltpu.SemaphoreType.DMA(())],
        compiler_params=_params(("arbitrary",) * 3),
        name="ffn",
    )(idx3, gate3, h2, w_gate, w_up, w_down)


_WIN = LANES
_ALIGN = 16
_CHUNK = _WIN - _ALIGN


def _combine_kernel(base_ref, cnt_ref, pos_ref, x1_ref, gt_ref, g_ref, ye_hbm, o_ref, stage, y_scr, sems,
                    *, cap):
    tt = pos_ref.shape[0]
    r_io = lax.broadcasted_iota(jnp.int32, (tt, _WIN), 1)

    def window(e, lo):
        aligned = lax.shift_left(lax.shift_right_logical(lo, 4), 4)
        return pl.multiple_of(jnp.minimum(aligned, cap - _WIN), _ALIGN)

    def copy(e, ws, slot_sem):
        return pltpu.make_async_copy(ye_hbm.at[e, pl.ds(ws, _WIN), :], stage.at[e], sems.at[slot_sem])

    def onehot(e, ws, lo):
        p = jnp.broadcast_to(pos_ref[:, e:e + 1], (tt, _WIN))
        hit = ((p - ws) == r_io) & (p >= lo) & (p < lo + _CHUNK)
        return jnp.where(hit, 1.0, 0.0).astype(_bf16)

    starts = []
    for e in range(N_EXPERTS):
        lo = base_ref[0, 0, e]
        ws = window(e, lo)
        copy(e, ws, e).start()
        starts.append((lo, ws))
    oh = jnp.concatenate([onehot(e, ws, lo) for e, (lo, ws) in enumerate(starts)], axis=1)
    for e, (lo, ws) in enumerate(starts):
        copy(e, ws, e).wait()
    d = stage.shape[2]
    y_scr[...] = _dot(oh, stage[...].reshape(N_EXPERTS * _WIN, d))

    for e in range(N_EXPERTS):
        base = base_ref[0, 0, e]
        n_chunks = (cnt_ref[0, 0, e] + _CHUNK - 1) // _CHUNK

        def extra(kc, carry):
            lo = base + kc * _CHUNK
            ws = window(e, lo)
            cp = copy(e, ws, e)
            cp.start()
            cp.wait()
            y_scr[...] += _dot(onehot(e, ws, lo), stage[e])
            return carry

        lax.fori_loop(1, n_chunks, extra, 0)

    o_ref[0] = x1_ref[0] + gt_ref[0] * _rms(y_scr[...], g_ref[...])


def _combine(base, cnt, pos, x1, gt, g, ye, cap):
    b, s, d = x1.shape
    tt = min(512, s)
    nt = s // tt
    mod = pl.BlockSpec((1, 1, d), lambda bi, i: (bi, 0, 0))
    smem = pl.BlockSpec((1, 1, N_EXPERTS), lambda bi, i: (bi * nt + i, 0, 0), memory_space=pltpu.SMEM)
    return pl.pallas_call(
        functools.partial(_combine_kernel, cap=cap),
        grid=(b, nt),
        in_specs=[smem, smem,
                  pl.BlockSpec((tt, N_EXPERTS), lambda bi, i: (bi * nt + i, 0)),
                  pl.BlockSpec((1, tt, d), lambda bi, i: (bi, i, 0)), mod, _const_spec(g.shape),
                  pl.BlockSpec(memory_space=pl.ANY)],
        out_specs=pl.BlockSpec((1, tt, d), lambda bi, i: (bi, i, 0)),
        out_shape=jax.ShapeDtypeStruct((b, s, d), _f32),
        scratch_shapes=[pltpu.VMEM((N_EXPERTS, _WIN, d), _bf16), pltpu.VMEM((tt, d), _f32),
                        pltpu.SemaphoreType.DMA((N_EXPERTS,))],
        compiler_params=_params(("arbitrary", "arbitrary")),
        name="combine",
    )(base, cnt, pos, x1, gt, g, ye)


def _pair_perm():
    blk = jnp.concatenate([jnp.arange(0, 32), jnp.arange(64, 96), jnp.arange(32, 64), jnp.arange(96, 128)])
    return (jnp.arange(HEADS)[:, None] * LANES + blk[None, :]).reshape(-1)


def _pad_rope_cols(w):
    k = w.shape[0]
    w = w.reshape(k, -1, 2, 32)
    w = jnp.concatenate([w, jnp.zeros_like(w)], axis=-1)
    return w.reshape(k, -1)


def _prep_weights(w_in, w_uq, w_ukv, w_out, w_router):
    perm = _pair_perm()
    c = GROUP_W
    w1 = jnp.concatenate([
        w_in[:, 0:c][:, perm], w_in[:, c:2 * c][:, perm], w_in[:, 2 * c:3 * c],
        w_in[:, 3 * c:3 * c + Q_LORA], w_in[:, 3 * c + Q_LORA:3 * c + Q_LORA + KV_LORA],
        _pad_rope_cols(w_in[:, 3 * c + Q_LORA + KV_LORA:])], axis=1).astype(_bf16)
    wq = w_uq.reshape(Q_LORA, HEADS, MLA_QK)
    wqn = wq[:, :, :HEAD_W].reshape(Q_LORA, GROUP_W).astype(_bf16)
    wqr = _pad_rope_cols(wq[:, :, HEAD_W:].reshape(Q_LORA, HEADS * MLA_ROPE)).astype(_bf16)
    wkv = w_ukv.astype(_bf16)
    woa = w_out[:GROUP_W].astype(_bf16)
    wob = w_out[GROUP_W:].astype(_bf16)
    wr = jnp.pad(w_router, ((0, 0), (0, LANES - N_EXPERTS)))
    wrh, wrl = _split(wr)
    return w1, wqn, wqr, wkv, woa, wob, wrh, wrl


def _rope_tables(s):
    half = DIFF_QK // 2
    inv = ROPE_THETA ** (-jnp.arange(half, dtype=_f32) / half)
    ang = jnp.arange(s, dtype=_f32)[:, None] * inv[None, :]
    cos = jnp.cos(ang)
    sin = jnp.sin(ang)
    return jnp.tile(cos, (1, 4)), jnp.concatenate([-sin, -sin, sin, sin], axis=1)


def _layer(x, mod, wts, lam_init):
    (g_pre_attn, g_post_attn, g_pre_ffn, g_post_ffn, lq1, lk1, lq2, lk2, g_diff_sub, g_q_lat, g_kv_lat,
     g_mla_out, w1, wqn, wqr, wkv, woa, wob, wrh, wrl, w_gate, w_up, w_down) = wts
    b, s, d = x.shape
    n = b * s
    cap = CAPACITY_FACTOR * n // N_EXPERTS
    sh_a, sc_a, gt_a, sh_f, sc_f, gt_f = [mod[:, None, i * d:(i + 1) * d] for i in range(6)]
    cos, sin = _rope_tables(s)

    qd, kd, vd, qn, qr, knv, kpe = _proj(x, sc_a, sh_a, g_pre_attn, w1, g_q_lat, g_kv_lat, wqn, wqr, wkv,
                                         cos, sin)
    oa = _dattn(qd, kd, vd, lq1, lk1, lq2, lk2, g_diff_sub, lam_init)
    ob = _mattn(qn, qr, knv, kpe)
    x1, h2, aff_t = _post(x, oa, ob, gt_a, sc_f, sh_f, g_mla_out, g_post_attn, g_pre_ffn, woa, wob, wrh, wrl)

    n_pad = ROUTE_BLOCKS * LANES
    aff3 = jnp.pad(aff_t, ((0, 0), (0, n_pad - n))).reshape(N_EXPERTS, ROUTE_BLOCKS, LANES)
    idx8, gate8, selpos, bex8 = _route(aff3, cap, n)

    tc = min(1024, cap)
    idx3 = idx8[:, 0, :].reshape(N_EXPERTS * (cap // tc), 1, tc)
    gate3 = gate8[:, 0, :].reshape(N_EXPERTS, cap, 1)
    ye = _ffn(idx3, gate3, h2.reshape(n, d), w_gate, w_up, w_down, cap)

    tt = min(512, s)
    n_tiles = n // tt
    bex = bex8[:, 0, :].astype(jnp.int32).T
    base = bex[::tt // LANES][:n_tiles]
    cnt = jnp.concatenate([base[1:], jnp.full((1, N_EXPERTS), cap, jnp.int32)], axis=0) - base
    pos = selpos.reshape(N_EXPERTS, n_pad).T[:n]
    return _combine(base.reshape(n_tiles, 1, N_EXPERTS), cnt.reshape(n_tiles, 1, N_EXPERTS), pos, x1, gt_f,
                    g_post_ffn, ye, cap)


def kernel(x_prompt, x_sample, c_prompt, c_sample, w_ada, b_ada, g_pre_attn, g_post_attn, g_pre_ffn, g_post_ffn, w_in, lam_q1, lam_k1, lam_q2, lam_k2, g_diff_sub, g_q_lat, w_uq, g_kv_lat, w_ukv, g_mla_out, w_out, w_router, w_gate, w_up, w_down):
    y_prompt, y_sample = x_prompt, x_sample
    bp = c_prompt.shape[0]
    bs = c_sample.shape[0]
    rows = -(-(bp + bs) // 8) * 8
    c_all = jnp.concatenate([c_prompt, c_sample, jnp.zeros((rows - bp - bs, c_prompt.shape[1]), _f32)], axis=0)
    for l in range(DEPTH):
        lam_init = 0.8 - 0.6 * math.exp(-0.3 * l)
        mod = _ada(c_all, w_ada[l], b_ada[l][None, :])
        row = lambda a: a[l][None, :]
        wts = (row(g_pre_attn), row(g_post_attn), row(g_pre_ffn), row(g_post_ffn), row(lam_q1), row(lam_k1),
               row(lam_q2), row(lam_k2), row(g_diff_sub), row(g_q_lat), row(g_kv_lat), row(g_mla_out),
               *_prep_weights(w_in[l], w_uq[l], w_ukv[l], w_out[l], w_router[l]),
               w_gate[l], w_up[l], w_down[l])
        y_prompt = _layer(y_prompt, mod[:bp], wts, lam_init)
        y_sample = _layer(y_sample, mod[bp:bp + bs], wts, lam_init)
    return (y_prompt, y_sample)
```

```python
import functools
import math

import jax
import jax.numpy as jnp
from jax import lax
from jax.experimental import pallas as pl
from jax.experimental.pallas import tpu as pltpu

D_MODEL = 2048
DEPTH = 1
ROPE_THETA = 10000.0
NORM_EPS = 1e-6

HEADS = D_MODEL // 256
DIFF_QK = 64
HEAD_W = 128
MLA_ROPE = 64
MLA_QK = HEAD_W + MLA_ROPE
Q_LORA = D_MODEL // 4
KV_LORA = D_MODEL // 8
GROUP_W = HEADS * HEAD_W
N_EXPERTS = 16
CAPACITY_FACTOR = 2
D_FF = D_MODEL

LANES = 128
ROUTE_BLOCKS = 128
VMEM_LIMIT = 56 * 1024 * 1024

_O_QD = 0
_O_KD = _O_QD + GROUP_W
_O_VD = _O_KD + GROUP_W
_O_MQ = _O_VD + GROUP_W
_O_MKV = _O_MQ + Q_LORA
_O_KR = _O_MKV + KV_LORA
_W1_COLS = _O_KR + LANES

_f32 = jnp.float32
_bf16 = jnp.bfloat16


def _dot(a, b):
    return jnp.dot(a, b, preferred_element_type=_f32)


def _dot_nt(a, b):
    return lax.dot_general(a, b, (((1,), (1,)), ((), ())), preferred_element_type=_f32)


def _split(x):
    hi = x.astype(_bf16)
    lo = (x - hi.astype(_f32)).astype(_bf16)
    return hi, lo


def _dot3(a, b):
    ah, al = _split(a)
    bh, bl = _split(b)
    return _dot(ah, bh) + _dot(al, bh) + _dot(ah, bl)


def _rms(x, g):
    return x * lax.rsqrt(jnp.mean(x * x, axis=-1, keepdims=True) + NORM_EPS) * g


def _rope(x, cos, sin):
    return x * cos + pltpu.roll(x, 64, 1) * sin


def _params(sem, vmem=VMEM_LIMIT):
    return pltpu.CompilerParams(dimension_semantics=sem, vmem_limit_bytes=vmem)


def _const_spec(shape):
    nd = len(shape)
    return pl.BlockSpec(shape, lambda *_: (0,) * nd, pipeline_mode=pl.Buffered(1))


def _ada_kernel(c_ref, w_ref, b_ref, o_ref):
    c = c_ref[...]
    s = c * (1.0 / (1.0 + jnp.exp(-c)))
    o_ref[...] = _dot3(s, w_ref[...]) + b_ref[...]


def _ada(c, w, b):
    rows, d = c.shape
    cols = w.shape[1]
    tn = 1024
    return pl.pallas_call(
        _ada_kernel,
        grid=(cols // tn,),
        in_specs=[pl.BlockSpec((rows, d), lambda j: (0, 0)),
                  pl.BlockSpec((d, tn), lambda j: (0, j)),
                  pl.BlockSpec((1, tn), lambda j: (0, j))],
        out_specs=pl.BlockSpec((rows, tn), lambda j: (0, j)),
        out_shape=jax.ShapeDtypeStruct((rows, cols), _f32),
        compiler_params=_params(("arbitrary",)),
        name="ada",
    )(c, w, b)


def _proj_kernel(x_ref, sc_ref, sh_ref, g_ref, w1_ref, gq_ref, gkv_ref, wqn_ref, wqr_ref, wkv_ref,
                 cos_ref, sin_ref, qd_ref, kd_ref, vd_ref, qn_ref, qr_ref, knv_ref, kpe_ref):
    x = x_ref[0]
    h = _rms(x, g_ref[...]) * (1.0 + sc_ref[0]) + sh_ref[0]
    hb = h.astype(_bf16)
    cos = cos_ref[...]
    sin = sin_ref[...]

    q = _dot(hb, w1_ref[:, _O_QD:_O_QD + GROUP_W])
    for hd in range(HEADS):
        sl = slice(hd * LANES, (hd + 1) * LANES)
        qd_ref[0, :, sl] = (_rope(q[:, sl], cos, sin) * (DIFF_QK ** -0.5)).astype(_bf16)
    k = _dot(hb, w1_ref[:, _O_KD:_O_KD + GROUP_W])
    for hd in range(HEADS):
        sl = slice(hd * LANES, (hd + 1) * LANES)
        kd_ref[0, :, sl] = _rope(k[:, sl], cos, sin).astype(_bf16)
    vd_ref[0] = _dot(hb, w1_ref[:, _O_VD:_O_VD + GROUP_W]).astype(_bf16)

    ql = _rms(_dot(hb, w1_ref[:, _O_MQ:_O_MQ + Q_LORA]), gq_ref[...]).astype(_bf16)
    mscale = MLA_QK ** -0.5
    qn_ref[0] = (_dot(ql, wqn_ref[...]) * mscale).astype(_bf16)
    qr = _dot(ql, wqr_ref[...])
    for hd in range(HEADS):
        sl = slice(hd * LANES, (hd + 1) * LANES)
        qr_ref[0, :, sl] = (_rope(qr[:, sl], cos, sin) * mscale).astype(_bf16)

    kvl = _rms(_dot(hb, w1_ref[:, _O_MKV:_O_MKV + KV_LORA]), gkv_ref[...]).astype(_bf16)
    knv_ref[0] = _dot(kvl, wkv_ref[...]).astype(_bf16)
    kpe_ref[0] = _rope(_dot(hb, w1_ref[:, _O_KR:_O_KR + LANES]), cos, sin).astype(_bf16)


def _proj(x, sc, sh, g, w1, gq, gkv, wqn, wqr, wkv, cos, sin):
    b, s, d = x.shape
    tm = min(256, s)
    row = lambda width: pl.BlockSpec((1, tm, width), lambda bi, i: (bi, i, 0))
    mod = pl.BlockSpec((1, 1, d), lambda bi, i: (bi, 0, 0))
    tab = pl.BlockSpec((tm, LANES), lambda bi, i: (i, 0))
    widths = (GROUP_W, GROUP_W, GROUP_W, GROUP_W, GROUP_W, 2 * GROUP_W, LANES)
    return pl.pallas_call(
        _proj_kernel,
        grid=(b, s // tm),
        in_specs=[row(d), mod, mod, _const_spec(g.shape), _const_spec(w1.shape), _const_spec(gq.shape),
                  _const_spec(gkv.shape), _const_spec(wqn.shape), _const_spec(wqr.shape),
                  _const_spec(wkv.shape), tab, tab],
        out_specs=[row(w) for w in widths],
        out_shape=[jax.ShapeDtypeStruct((b, s, w), _bf16) for w in widths],
        compiler_params=_params(("arbitrary", "arbitrary")),
        name="proj",
    )(x, sc, sh, g, w1, gq, gkv, wqn, wqr, wkv, cos, sin)


def _softmax_pv(s, v):
    m = jnp.max(s, axis=-1, keepdims=True)
    p = jnp.exp(s - m)
    l = jnp.sum(p, axis=-1, keepdims=True)
    return _dot(p.astype(_bf16), v) / l


def _dattn_kernel(lq1_ref, lk1_ref, lq2_ref, lk2_ref, g_ref, q_ref, k_ref, v_ref, o_ref, *, lam_init):
    q = q_ref[0]
    k = k_ref[0]
    v = v_ref[0]
    lane = lax.broadcasted_iota(jnp.int32, q.shape, 1)
    first = (lane & 63) < 32
    zero = jnp.zeros_like(q)
    o1 = _softmax_pv(_dot_nt(jnp.where(first, q, zero), k), v)
    o2 = _softmax_pv(_dot_nt(jnp.where(first, zero, q), k), v)
    lam = (jnp.exp(jnp.sum(lq1_ref[...] * lk1_ref[...], axis=-1, keepdims=True))
           - jnp.exp(jnp.sum(lq2_ref[...] * lk2_ref[...], axis=-1, keepdims=True)) + lam_init)
    o = o1 - lam * o2
    o_ref[0] = (_rms(o, g_ref[...]) * (1.0 - lam_init)).astype(_bf16)


def _dattn(qd, kd, vd, lq1, lk1, lq2, lk2, g, lam_init):
    b, s, _ = qd.shape
    tq = min(256, s)
    small = lambda a: pl.BlockSpec(a.shape, lambda bi, h, i: (0, 0))
    return pl.pallas_call(
        functools.partial(_dattn_kernel, lam_init=lam_init),
        grid=(b, HEADS, s // tq),
        in_specs=[small(lq1), small(lk1), small(lq2), small(lk2), small(g),
                  pl.BlockSpec((1, tq, LANES), lambda bi, h, i: (bi, i, h)),
                  pl.BlockSpec((1, s, LANES), lambda bi, h, i: (bi, 0, h)),
                  pl.BlockSpec((1, s, LANES), lambda bi, h, i: (bi, 0, h))],
        out_specs=pl.BlockSpec((1, tq, LANES), lambda bi, h, i: (bi, i, h)),
        out_shape=jax.ShapeDtypeStruct((b, s, GROUP_W), _bf16),
        compiler_params=_params(("arbitrary",) * 3),
        name="dattn",
    )(lq1, lk1, lq2, lk2, g, qd, kd, vd)


def _mattn_kernel(qn_ref, qr_ref, kn_ref, v_ref, kpe_ref, o_ref):
    q = jnp.concatenate([qn_ref[0], qr_ref[0]], axis=-1)
    k = jnp.concatenate([kn_ref[0], kpe_ref[0]], axis=-1)
    o_ref[0] = _softmax_pv(_dot_nt(q, k), v_ref[0]).astype(_bf16)


def _mattn(qn, qr, knv, kpe):
    b, s, _ = qn.shape
    tq = min(256, s)
    return pl.pallas_call(
        _mattn_kernel,
        grid=(b, HEADS, s // tq),
        in_specs=[pl.BlockSpec((1, tq, LANES), lambda bi, h, i: (bi, i, h)),
                  pl.BlockSpec((1, tq, LANES), lambda bi, h, i: (bi, i, h)),
                  pl.BlockSpec((1, s, LANES), lambda bi, h, i: (bi, 0, 2 * h)),
                  pl.BlockSpec((1, s, LANES), lambda bi, h, i: (bi, 0, 2 * h + 1)),
                  pl.BlockSpec((1, s, LANES), lambda bi, h, i: (bi, 0, 0))],
        out_specs=pl.BlockSpec((1, tq, LANES), lambda bi, h, i: (bi, i, h)),
        out_shape=jax.ShapeDtypeStruct((b, s, GROUP_W), _bf16),
        compiler_params=_params(("arbitrary",) * 3),
        name="mattn",
    )(qn, qr, knv, knv, kpe)


def _post_kernel(x_ref, oa_ref, ob_ref, gt_ref, sc_ref, sh_ref, gmla_ref, gpost_ref, gpre_ref,
                 woa_ref, wob_ref, wrh_ref, wrl_ref, x1_ref, h2_ref, aff_ref):
    obn = _rms(ob_ref[0].astype(_f32), gmla_ref[...]).astype(_bf16)
    m = _dot(oa_ref[0], woa_ref[...]) + _dot(obn, wob_ref[...])
    x1 = x_ref[0] + gt_ref[0] * _rms(m, gpost_ref[...])
    x1_ref[0] = x1
    h2 = _rms(x1, gpre_ref[...]) * (1.0 + sc_ref[0]) + sh_ref[0]
    hbits = pltpu.bitcast(h2.astype(_bf16).astype(_f32), jnp.uint32)
    half = hbits.shape[1] // 2
    h2_ref[0] = hbits[:, :half] | lax.shift_right_logical(hbits[:, half:], jnp.uint32(16))
    hh, hl = _split(h2)
    logits = _dot(hh, wrh_ref[...]) + _dot(hl, wrh_ref[...]) + _dot(hh, wrl_ref[...])
    lane = lax.broadcasted_iota(jnp.int32, logits.shape, 1)
    logits = jnp.where(lane < N_EXPERTS, logits, -jnp.inf)
    p = jnp.exp(logits - jnp.max(logits, axis=-1, keepdims=True))
    aff = p / jnp.sum(p, axis=-1, keepdims=True)
    aff_ref[...] = aff.T[0:N_EXPERTS, :]


def _post(x, oa, ob, gt, sc, sh, gmla, gpost, gpre, woa, wob, wrh, wrl):
    b, s, d = x.shape
    tm = min(256, s)
    nt = s // tm
    row = lambda width: pl.BlockSpec((1, tm, width), lambda bi, i: (bi, i, 0))
    mod = pl.BlockSpec((1, 1, d), lambda bi, i: (bi, 0, 0))
    return pl.pallas_call(
        _post_kernel,
        grid=(b, nt),
        in_specs=[row(d), row(GROUP_W), row(GROUP_W), mod, mod, mod, _const_spec(gmla.shape),
                  _const_spec(gpost.shape), _const_spec(gpre.shape), _const_spec(woa.shape),
                  _const_spec(wob.shape), _const_spec(wrh.shape), _const_spec(wrl.shape)],
        out_specs=[row(d), row(d // 2), pl.BlockSpec((N_EXPERTS, tm), lambda bi, i: (0, bi * nt + i))],
        out_shape=[jax.ShapeDtypeStruct((b, s, d), _f32), jax.ShapeDtypeStruct((b, s, d // 2), jnp.uint32),
                   jax.ShapeDtypeStruct((N_EXPERTS, b * s), _f32)],
        compiler_params=_params(("arbitrary", "arbitrary")),
        name="post",
    )(x, oa, ob, gt, sc, sh, gmla, gpost, gpre, woa, wob, wrh, wrl)


def _route_kernel(aff_ref, idx_ref, gate_ref, selpos_ref, bex_ref, gt_scr, eq_scr, need_scr, *, cap, n_tok):
    nb = ROUTE_BLOCKS
    bits = pltpu.bitcast(aff_ref[...], jnp.int32)

    def count(mask):
        c = jnp.sum(jnp.where(mask, 1.0, 0.0), axis=2, keepdims=True)
        return jnp.sum(c, axis=1, keepdims=True)

    def search(i, thr):
        cand = thr | jnp.left_shift(jnp.int32(1), 30 - i)
        return jnp.where(count(bits >= cand) >= cap, cand, thr)

    thr = lax.fori_loop(0, 31, search, jnp.zeros((N_EXPERTS, 1, 1), jnp.int32))
    gt = bits > thr
    gt_scr[...] = jnp.where(gt, 1.0, 0.0)
    eq_scr[...] = jnp.where(bits == thr, 1.0, 0.0)
    need_scr[...] = jnp.broadcast_to(cap - count(gt), need_scr.shape)

    r_io = lax.broadcasted_iota(jnp.int32, (nb, LANES), 0)
    c_io = lax.broadcasted_iota(jnp.int32, (nb, LANES), 1)
    triu = jnp.where(r_io <= c_io, 1.0, 0.0).astype(_bf16)
    tril = jnp.where(c_io <= r_io, 1.0, 0.0).astype(_bf16)
    reps = cap // LANES
    slot = lax.broadcasted_iota(jnp.int32, (nb, cap), 1).astype(_f32)
    row = lax.broadcasted_iota(jnp.int32, (nb, cap), 0).astype(_f32)

    def prefix(m):
        loc = _dot(m.astype(_bf16), triu)
        tot = jnp.broadcast_to(loc[:, LANES - 1:LANES], loc.shape)
        return loc, tot, _dot(tril, tot.astype(_bf16))

    def per_expert(e, carry):
        gte = gt_scr[e]
        eqe = eq_scr[e]
        need = need_scr[e][0:1, :]
        eloc, etot, ebinc = prefix(eqe)
        eq_before = ebinc - etot + eloc - eqe
        sel = jnp.maximum(gte, eqe * jnp.where(eq_before < need, 1.0, 0.0))
        loc, tot, binc = prefix(sel)
        bexc = binc - tot
        pos = bexc + loc - sel
        selpos_ref[e] = jnp.where(sel > 0.0, pos, -1.0).astype(jnp.int32)
        bex_ref[e] = bexc.T[0:8, :]

        binc_t = jnp.tile(binc, (1, reps))
        bexc_t = jnp.tile(bexc, (1, reps))
        blk = jnp.sum(jnp.where(binc_t <= slot, 1.0, 0.0), axis=0, keepdims=True)
        hit = row == blk
        onehot = jnp.where(hit, 1.0, 0.0).astype(_bf16)
        rank = slot[0:1, :] - jnp.sum(jnp.where(hit, bexc_t, 0.0), axis=0, keepdims=True)
        loc_rows = _dot(loc.T.astype(_bf16), onehot)
        within = jnp.sum(jnp.where(loc_rows <= rank, 1.0, 0.0), axis=0, keepdims=True)
        token = jnp.minimum(blk * LANES + within, n_tok - 1.0)
        idx_ref[e] = jnp.broadcast_to(token, (8, cap)).astype(jnp.int32)

        at = aff_ref[e].T
        a0 = at.astype(_bf16)
        r1 = at - a0.astype(_f32)
        a1 = r1.astype(_bf16)
        a2 = (r1 - a1.astype(_f32)).astype(_bf16)
        aff_rows = _dot(a0, onehot) + _dot(a1, onehot) + _dot(a2, onehot)
        gate = jnp.sum(jnp.where(row == within, aff_rows, 0.0), axis=0, keepdims=True)
        gate_ref[e] = jnp.broadcast_to(gate, (8, cap))
        return carry

    lax.fori_loop(0, N_EXPERTS, per_expert, 0)


def _route(aff3, cap, n_tok):
    nb = ROUTE_BLOCKS
    full = lambda shape: pl.BlockSpec(shape, lambda: (0,) * len(shape))
    return pl.pallas_call(
        functools.partial(_route_kernel, cap=cap, n_tok=n_tok),
        in_specs=[full((N_EXPERTS, nb, LANES))],
        out_specs=[full((N_EXPERTS, 8, cap)), full((N_EXPERTS, 8, cap)), full((N_EXPERTS, nb, LANES)),
                   full((N_EXPERTS, 8, LANES))],
        out_shape=[jax.ShapeDtypeStruct((N_EXPERTS, 8, cap), jnp.int32),
                   jax.ShapeDtypeStruct((N_EXPERTS, 8, cap), _f32),
                   jax.ShapeDtypeStruct((N_EXPERTS, nb, LANES), jnp.int32),
                   jax.ShapeDtypeStruct((N_EXPERTS, 8, LANES), _f32)],
        scratch_shapes=[pltpu.VMEM((N_EXPERTS, nb, LANES), _f32), pltpu.VMEM((N_EXPERTS, nb, LANES), _f32),
                        pltpu.VMEM((N_EXPERTS, 8, LANES), _f32)],
        compiler_params=pltpu.CompilerParams(vmem_limit_bytes=VMEM_LIMIT),
        name="route",
    )(aff3)


def _ffn_kernel(idx0_ref, idxn_ref, gate_ref, h_hbm, wg_ref, wu_ref, wd_ref, o_ref, rows_scr, xe_scr, acc_scr,
                sems, *, tc, n_tiles, nf):
    f = pl.program_id(2)
    g = pl.program_id(0) * pl.num_programs(1) + pl.program_id(1)
    slot = g & 1
    half = rows_scr.shape[2]
    per_step = tc // nf

    def row_copy(idx_ref, r, dst_slot):
        t = idx_ref[0, 0, r]
        return pltpu.make_async_copy(h_hbm.at[pl.ds(t, 1), :], rows_scr.at[dst_slot, pl.ds(r, 1), :],
                                     sems.at[dst_slot])

    @pl.when((g == 0) & (f == 0))
    def _():
        def issue(r, carry):
            row_copy(idx0_ref, r, 0).start()
            return carry

        lax.fori_loop(0, tc, issue, 0)

    @pl.when(f == 0)
    def _():
        pltpu.make_async_copy(h_hbm.at[pl.ds(0, tc), :], rows_scr.at[slot], sems.at[slot]).wait()
        u = rows_scr[slot]
        hi = pltpu.bitcast(u & jnp.uint32(0xFFFF0000), _f32)
        lo = pltpu.bitcast(lax.shift_left(u, jnp.uint32(16)), _f32)
        xe_scr[:, :half] = hi.astype(_bf16)
        xe_scr[:, half:] = lo.astype(_bf16)
        acc_scr[...] = jnp.zeros_like(acc_scr)

    xe = xe_scr[...]
    hg = _dot(xe, wg_ref[...].astype(_bf16))
    hu = _dot(xe, wu_ref[...].astype(_bf16))
    a = (hg * (1.0 / (1.0 + jnp.exp(-hg))) * hu).astype(_bf16)
    acc_scr[...] += _dot(a, wd_ref[...].astype(_bf16))

    @pl.when(g + 1 < n_tiles)
    def _():
        for r in range(per_step):
            row_copy(idxn_ref, f * per_step + r, 1 - slot).start()

    @pl.when(f == nf - 1)
    def _():
        o_ref[0] = (acc_scr[...] * gate_ref[0]).astype(_bf16)


def _ffn(idx3, gate3, h2p, w_gate, w_up, w_down, cap):
    n, half = h2p.shape
    d = 2 * half
    tc = min(1024, cap)
    nc = cap // tc
    tf = min(256, D_FF)
    n_tiles = N_EXPERTS * nc
    idx_spec = lambda off: pl.BlockSpec(
        (1, 1, tc), lambda e, i, f: (jnp.minimum(e * nc + i + off, n_tiles - 1), 0, 0), memory_space=pltpu.SMEM)
    return pl.pallas_call(
        functools.partial(_ffn_kernel, tc=tc, n_tiles=n_tiles, nf=D_FF // tf),
        grid=(N_EXPERTS, nc, D_FF // tf),
        in_specs=[idx_spec(0), idx_spec(1),
                  pl.BlockSpec((1, tc, 1), lambda e, i, f: (e, i, 0)),
                  pl.BlockSpec(memory_space=pl.ANY),
                  pl.BlockSpec((None, d, tf), lambda e, i, f: (e, 0, f)),
                  pl.BlockSpec((None, d, tf), lambda e, i, f: (e, 0, f)),
                  pl.BlockSpec((None, tf, d), lambda e, i, f: (e, f, 0))],
        out_specs=pl.BlockSpec((1, tc, d), lambda e, i, f: (e, i, 0)),
        out_shape=jax.ShapeDtypeStruct((N_EXPERTS, cap, d), _bf16),
        scratch_shapes=[pltpu.VMEM((2, tc, half), jnp.uint32), pltpu.VMEM((tc, d), _bf16),
                        pltpu.VMEM((tc, d), _f32), pltpu.SemaphoreType.DMA((2,))],
        compiler_params=_params(("arbitrary",) * 3),
        name="ffn",
    )(idx3, idx3, gate3, h2p, w_gate, w_up, w_down)


_WIN = LANES
_ALIGN = 16
_CHUNK = _WIN - _ALIGN


def _combine_kernel(base_ref, basen_ref, cnt_ref, pos_ref, x1_ref, gt_ref, g_ref, ye_hbm, o_ref, stage, xstage,
                    y_scr, sems, xsem, *, cap):
    tt = pos_ref.shape[0]
    d = stage.shape[3]
    t = pl.program_id(0) * pl.num_programs(1) + pl.program_id(1)
    n_t = pl.num_programs(0) * pl.num_programs(1)
    slot = t & 1
    r_io = lax.broadcasted_iota(jnp.int32, (tt, _WIN), 1)

    def window(lo):
        aligned = lax.shift_left(lax.shift_right_logical(lo, 4), 4)
        return pl.multiple_of(jnp.minimum(aligned, cap - _WIN), _ALIGN)

    def copy(e, ws, dst_slot):
        return pltpu.make_async_copy(ye_hbm.at[e, pl.ds(ws, _WIN), :], stage.at[dst_slot, e], sems.at[dst_slot])

    def onehot(e, ws, lo):
        p = jnp.broadcast_to(pos_ref[:, e:e + 1], (tt, _WIN))
        hit = ((p - ws) == r_io) & (p >= lo) & (p < lo + _CHUNK)
        return jnp.where(hit, 1.0, 0.0).astype(_bf16)

    @pl.when(t == 0)
    def _():
        for e in range(N_EXPERTS):
            copy(e, window(base_ref[0, 0, e]), 0).start()

    @pl.when(t + 1 < n_t)
    def _():
        for e in range(N_EXPERTS):
            copy(e, window(basen_ref[0, 0, e]), 1 - slot).start()

    los = [base_ref[0, 0, e] for e in range(N_EXPERTS)]
    oh = jnp.concatenate([onehot(e, window(lo), lo) for e, lo in enumerate(los)], axis=1)
    for e, lo in enumerate(los):
        copy(e, window(lo), slot).wait()
    y_scr[...] = _dot(oh, stage[slot].reshape(N_EXPERTS * _WIN, d))

    for e in range(N_EXPERTS):
        n_chunks = (cnt_ref[0, 0, e] + _CHUNK - 1) // _CHUNK

        def extra(kc, carry, e=e):
            lo = los[e] + kc * _CHUNK
            ws = window(lo)
            cp = pltpu.make_async_copy(ye_hbm.at[e, pl.ds(ws, _WIN), :], xstage, xsem)
            cp.start()
            cp.wait()
            y_scr[...] += _dot(onehot(e, ws, lo), xstage[...])
            return carry

        lax.fori_loop(1, n_chunks, extra, 0)

    o_ref[0] = x1_ref[0] + gt_ref[0] * _rms(y_scr[...], g_ref[...])


def _combine(base, cnt, pos, x1, gt, g, ye, cap):
    b, s, d = x1.shape
    tt = min(512, s)
    nt = s // tt
    n_t = b * nt
    mod = pl.BlockSpec((1, 1, d), lambda bi, i: (bi, 0, 0))
    smem = lambda off: pl.BlockSpec(
        (1, 1, N_EXPERTS), lambda bi, i: (jnp.minimum(bi * nt + i + off, n_t - 1), 0, 0), memory_space=pltpu.SMEM)
    return pl.pallas_call(
        functools.partial(_combine_kernel, cap=cap),
        grid=(b, nt),
        in_specs=[smem(0), smem(1), smem(0),
                  pl.BlockSpec((tt, N_EXPERTS), lambda bi, i: (bi * nt + i, 0)),
                  pl.BlockSpec((1, tt, d), lambda bi, i: (bi, i, 0)), mod, _const_spec(g.shape),
                  pl.BlockSpec(memory_space=pl.ANY)],
        out_specs=pl.BlockSpec((1, tt, d), lambda bi, i: (bi, i, 0)),
        out_shape=jax.ShapeDtypeStruct((b, s, d), _f32),
        scratch_shapes=[pltpu.VMEM((2, N_EXPERTS, _WIN, d), _bf16), pltpu.VMEM((_WIN, d), _bf16),
                        pltpu.VMEM((tt, d), _f32), pltpu.SemaphoreType.DMA((2,)), pltpu.SemaphoreType.DMA(())],
        compiler_params=_params(("arbitrary", "arbitrary")),
        name="combine",
    )(base, base, cnt, pos, x1, gt, g, ye)


def _pair_perm():
    blk = jnp.concatenate([jnp.arange(0, 32), jnp.arange(64, 96), jnp.arange(32, 64), jnp.arange(96, 128)])
    return (jnp.arange(HEADS)[:, None] * LANES + blk[None, :]).reshape(-1)


def _pad_rope_cols(w):
    k = w.shape[0]
    w = w.reshape(k, -1, 2, 32)
    w = jnp.concatenate([w, jnp.zeros_like(w)], axis=-1)
    return w.reshape(k, -1)


def _prep_weights(w_in, w_uq, w_ukv, w_out, w_router):
    perm = _pair_perm()
    c = GROUP_W
    w1 = jnp.concatenate([
        w_in[:, 0:c][:, perm], w_in[:, c:2 * c][:, perm], w_in[:, 2 * c:3 * c],
        w_in[:, 3 * c:3 * c + Q_LORA], w_in[:, 3 * c + Q_LORA:3 * c + Q_LORA + KV_LORA],
        _pad_rope_cols(w_in[:, 3 * c + Q_LORA + KV_LORA:])], axis=1).astype(_bf16)
    wq = w_uq.reshape(Q_LORA, HEADS, MLA_QK)
    wqn = wq[:, :, :HEAD_W].reshape(Q_LORA, GROUP_W).astype(_bf16)
    wqr = _pad_rope_cols(wq[:, :, HEAD_W:].reshape(Q_LORA, HEADS * MLA_ROPE)).astype(_bf16)
    wkv = w_ukv.astype(_bf16)
    woa = w_out[:GROUP_W].astype(_bf16)
    wob = w_out[GROUP_W:].astype(_bf16)
    wr = jnp.pad(w_router, ((0, 0), (0, LANES - N_EXPERTS)))
    wrh, wrl = _split(wr)
    return w1, wqn, wqr, wkv, woa, wob, wrh, wrl


def _rope_tables(s):
    half = DIFF_QK // 2
    inv = ROPE_THETA ** (-jnp.arange(half, dtype=_f32) / half)
    ang = jnp.arange(s, dtype=_f32)[:, None] * inv[None, :]
    cos = jnp.cos(ang)
    sin = jnp.sin(ang)
    return jnp.tile(cos, (1, 4)), jnp.concatenate([-sin, -sin, sin, sin], axis=1)


def _layer(x, mod, wts, lam_init):
    (g_pre_attn, g_post_attn, g_pre_ffn, g_post_ffn, lq1, lk1, lq2, lk2, g_diff_sub, g_q_lat, g_kv_lat,
     g_mla_out, w1, wqn, wqr, wkv, woa, wob, wrh, wrl, w_gate, w_up, w_down) = wts
    b, s, d = x.shape
    n = b * s
    cap = CAPACITY_FACTOR * n // N_EXPERTS
    sh_a, sc_a, gt_a, sh_f, sc_f, gt_f = [mod[:, None, i * d:(i + 1) * d] for i in range(6)]
    cos, sin = _rope_tables(s)

    qd, kd, vd, qn, qr, knv, kpe = _proj(x, sc_a, sh_a, g_pre_attn, w1, g_q_lat, g_kv_lat, wqn, wqr, wkv,
                                         cos, sin)
    oa = _dattn(qd, kd, vd, lq1, lk1, lq2, lk2, g_diff_sub, lam_init)
    ob = _mattn(qn, qr, knv, kpe)
    x1, h2, aff_t = _post(x, oa, ob, gt_a, sc_f, sh_f, g_mla_out, g_post_attn, g_pre_ffn, woa, wob, wrh, wrl)

    n_pad = ROUTE_BLOCKS * LANES
    aff3 = jnp.pad(aff_t, ((0, 0), (0, n_pad - n))).reshape(N_EXPERTS, ROUTE_BLOCKS, LANES)
    idx8, gate8, selpos, bex8 = _route(aff3, cap, n)

    tc = min(1024, cap)
    idx3 = idx8[:, 0, :].reshape(N_EXPERTS * (cap // tc), 1, tc)
    gate3 = gate8[:, 0, :].reshape(N_EXPERTS, cap, 1)
    ye = _ffn(idx3, gate3, h2.reshape(n, d // 2), w_gate, w_up, w_down, cap)

    tt = min(512, s)
    n_tiles = n // tt
    bex = bex8[:, 0, :].astype(jnp.int32).T
    base = bex[::tt // LANES][:n_tiles]
    cnt = jnp.concatenate([base[1:], jnp.full((1, N_EXPERTS), cap, jnp.int32)], axis=0) - base
    pos = selpos.reshape(N_EXPERTS, n_pad).T[:n]
    return _combine(base.reshape(n_tiles, 1, N_EXPERTS), cnt.reshape(n_tiles, 1, N_EXPERTS), pos, x1, gt_f,
                    g_post_ffn, ye, cap)


def kernel(x_prompt, x_sample, c_prompt, c_sample, w_ada, b_ada, g_pre_attn, g_post_attn, g_pre_ffn, g_post_ffn, w_in, lam_q1, lam_k1, lam_q2, lam_k2, g_diff_sub, g_q_lat, w_uq, g_kv_lat, w_ukv, g_mla_out, w_out, w_router, w_gate, w_up, w_down):
    y_prompt, y_sample = x_prompt, x_sample
    bp = c_prompt.shape[0]
    bs = c_sample.shape[0]
    rows = -(-(bp + bs) // 8) * 8
    c_all = jnp.concatenate([c_prompt, c_sample, jnp.zeros((rows - bp - bs, c_prompt.shape[1]), _f32)], axis=0)
    for l in range(DEPTH):
        lam_init = 0.8 - 0.6 * math.exp(-0.3 * l)
        mod = _ada(c_all, w_ada[l], b_ada[l][None, :])
        row = lambda a: a[l][None, :]
        wts = (row(g_pre_attn), row(g_post_attn), row(g_pre_ffn), row(g_post_ffn), row(lam_q1), row(lam_k1),
               row(lam_q2), row(lam_k2), row(g_diff_sub), row(g_q_lat), row(g_kv_lat), row(g_mla_out),
               *_prep_weights(w_in[l], w_uq[l], w_ukv[l], w_out[l], w_router[l]),
               w_gate[l], w_up[l], w_down[l])
        y_prompt = _layer(y_prompt, mod[:bp], wts, lam_init)
        y_sample = _layer(y_sample, mod[bp:bp + bs], wts, lam_init)
    return (y_prompt, y_sample)
```

```python
import functools
import math

import jax
import jax.numpy as jnp
from jax import lax
from jax.experimental import pallas as pl
from jax.experimental.pallas import tpu as pltpu

D_MODEL = 2048
DEPTH = 1
ROPE_THETA = 10000.0
NORM_EPS = 1e-6

HEADS = D_MODEL // 256
DIFF_QK = 64
HEAD_W = 128
MLA_ROPE = 64
MLA_QK = HEAD_W + MLA_ROPE
Q_LORA = D_MODEL // 4
KV_LORA = D_MODEL // 8
GROUP_W = HEADS * HEAD_W
N_EXPERTS = 16
CAPACITY_FACTOR = 2
D_FF = D_MODEL

LANES = 128
ROUTE_BLOCKS = 128
VMEM_LIMIT = 56 * 1024 * 1024
ATTN_TQ = 512
ATTN_KEY_CHUNK = 1024

_O_QD = 0
_O_KD = _O_QD + GROUP_W
_O_VD = _O_KD + GROUP_W
_O_MQ = _O_VD + GROUP_W
_O_MKV = _O_MQ + Q_LORA
_O_KR = _O_MKV + KV_LORA
_W1_COLS = _O_KR + LANES

_f32 = jnp.float32
_bf16 = jnp.bfloat16
_LOG2E = math.log2(math.e)


def _dot(a, b):
    return jnp.dot(a, b, preferred_element_type=_f32)


def _dot_nt(a, b):
    return lax.dot_general(a, b, (((1,), (1,)), ((), ())), preferred_element_type=_f32)


def _split(x):
    hi = x.astype(_bf16)
    lo = (x - hi.astype(_f32)).astype(_bf16)
    return hi, lo


def _dot3(a, b):
    ah, al = _split(a)
    bh, bl = _split(b)
    return _dot(ah, bh) + _dot(al, bh) + _dot(ah, bl)


def _rms(x, g):
    return x * lax.rsqrt(jnp.mean(x * x, axis=-1, keepdims=True) + NORM_EPS) * g


def _rope(x, cos, sin):
    return x * cos + pltpu.roll(x, 64, 1) * sin


def _params(sem, vmem=VMEM_LIMIT):
    return pltpu.CompilerParams(dimension_semantics=sem, vmem_limit_bytes=vmem)


def _const_spec(shape):
    nd = len(shape)
    return pl.BlockSpec(shape, lambda *_: (0,) * nd, pipeline_mode=pl.Buffered(1))


def _ada_kernel(c_ref, w_ref, b_ref, o_ref):
    c = c_ref[...]
    s = c * (1.0 / (1.0 + jnp.exp(-c)))
    o_ref[...] = _dot3(s, w_ref[...]) + b_ref[...]


def _ada(c, w, b):
    rows, d = c.shape
    cols = w.shape[1]
    tn = 1024
    return pl.pallas_call(
        _ada_kernel,
        grid=(cols // tn,),
        in_specs=[pl.BlockSpec((rows, d), lambda j: (0, 0)),
                  pl.BlockSpec((d, tn), lambda j: (0, j)),
                  pl.BlockSpec((1, tn), lambda j: (0, j))],
        out_specs=pl.BlockSpec((rows, tn), lambda j: (0, j)),
        out_shape=jax.ShapeDtypeStruct((rows, cols), _f32),
        compiler_params=_params(("arbitrary",)),
        name="ada",
    )(c, w, b)


def _proj_kernel(x_ref, sc_ref, sh_ref, g_ref, w1_ref, gq_ref, gkv_ref, wqn_ref, wqr_ref, wkv_ref,
                 cos_ref, sin_ref, qd_ref, kd_ref, vdt_ref, qn_ref, qr_ref, kn_ref, vmt_ref, kpe_ref):
    x = x_ref[0]
    h = _rms(x, g_ref[...]) * (1.0 + sc_ref[0]) + sh_ref[0]
    hb = h.astype(_bf16)
    cos = cos_ref[...]
    sin = sin_ref[...]

    q = _dot(hb, w1_ref[:, _O_QD:_O_QD + GROUP_W])
    for hd in range(HEADS):
        sl = slice(hd * LANES, (hd + 1) * LANES)
        qd_ref[0, :, sl] = (_rope(q[:, sl], cos, sin) * (DIFF_QK ** -0.5 * _LOG2E)).astype(_bf16)
    k = _dot(hb, w1_ref[:, _O_KD:_O_KD + GROUP_W])
    for hd in range(HEADS):
        sl = slice(hd * LANES, (hd + 1) * LANES)
        kd_ref[0, :, sl] = _rope(k[:, sl], cos, sin).astype(_bf16)
    vdt_ref[0] = _dot(hb, w1_ref[:, _O_VD:_O_VD + GROUP_W]).T.astype(_bf16)

    ql = _rms(_dot(hb, w1_ref[:, _O_MQ:_O_MQ + Q_LORA]), gq_ref[...]).astype(_bf16)
    mscale = MLA_QK ** -0.5 * _LOG2E
    qn_ref[0] = (_dot(ql, wqn_ref[...]) * mscale).astype(_bf16)
    qr = _dot(ql, wqr_ref[...])
    for hd in range(HEADS):
        sl = slice(hd * LANES, (hd + 1) * LANES)
        qr_ref[0, :, sl] = (_rope(qr[:, sl], cos, sin) * mscale).astype(_bf16)

    kvl = _rms(_dot(hb, w1_ref[:, _O_MKV:_O_MKV + KV_LORA]), gkv_ref[...]).astype(_bf16)
    kv = _dot(kvl, wkv_ref[...])
    kn_ref[0] = kv[:, :GROUP_W].astype(_bf16)
    vmt_ref[0] = kv[:, GROUP_W:].T.astype(_bf16)
    kpe_ref[0] = _rope(_dot(hb, w1_ref[:, _O_KR:_O_KR + LANES]), cos, sin).astype(_bf16)


def _proj(x, sc, sh, g, w1, gq, gkv, wqn, wqr, wkv, cos, sin):
    b, s, d = x.shape
    tm = min(256, s)
    row = lambda width: pl.BlockSpec((1, tm, width), lambda bi, i: (bi, i, 0))
    mod = pl.BlockSpec((1, 1, d), lambda bi, i: (bi, 0, 0))
    tab = pl.BlockSpec((tm, LANES), lambda bi, i: (i, 0))
    col = pl.BlockSpec((1, GROUP_W, tm), lambda bi, i: (bi, 0, i))
    rows = jax.ShapeDtypeStruct((b, s, GROUP_W), _bf16)
    cols = jax.ShapeDtypeStruct((b, GROUP_W, s), _bf16)
    return pl.pallas_call(
        _proj_kernel,
        grid=(b, s // tm),
        in_specs=[row(d), mod, mod, _const_spec(g.shape), _const_spec(w1.shape), _const_spec(gq.shape),
                  _const_spec(gkv.shape), _const_spec(wqn.shape), _const_spec(wqr.shape),
                  _const_spec(wkv.shape), tab, tab],
        out_specs=[row(GROUP_W), row(GROUP_W), col, row(GROUP_W), row(GROUP_W), row(GROUP_W), col, row(LANES)],
        out_shape=[rows, rows, cols, rows, rows, rows, cols, jax.ShapeDtypeStruct((b, s, LANES), _bf16)],
        compiler_params=_params(("arbitrary", "arbitrary")),
        name="proj",
    )(x, sc, sh, g, w1, gq, gkv, wqn, wqr, wkv, cos, sin)


def _softmax_pv_t(k, q, vt):
    s = k.shape[0]
    ck = min(ATTN_KEY_CHUNK, s)
    m = l = o = None
    n = s // ck
    scores = lambda c: _dot_nt(k[c * ck:(c + 1) * ck], q)
    st_next = scores(0)
    for c in range(n):
        st = st_next
        if c + 1 < n:
            st_next = scores(c + 1)
        mc = jnp.max(st, axis=0, keepdims=True)
        m_new = mc if c == 0 else jnp.maximum(m, mc)
        p = jnp.exp2(st - m_new)
        lc = jnp.sum(p, axis=0, keepdims=True)
        oc = _dot(vt[:, c * ck:(c + 1) * ck], p.astype(_bf16))
        if c == 0:
            l, o = lc, oc
        else:
            a = jnp.exp2(m - m_new)
            l = l * a + lc
            o = o * a + oc
        m = m_new
    return o * (1.0 / l)


def _dattn_kernel(lq1_ref, lk1_ref, lq2_ref, lk2_ref, g_ref, q_ref, k_ref, vt_ref, o_ref, *, lam_init):
    q = q_ref[0]
    k = k_ref[0]
    vt = vt_ref[0]
    lane = lax.broadcasted_iota(jnp.int32, q.shape, 1)
    first = (lane & 63) < 32
    zero = jnp.zeros_like(q)
    tq = q.shape[0]
    o12 = _softmax_pv_t(k, jnp.concatenate([jnp.where(first, q, zero), jnp.where(first, zero, q)], axis=0), vt)
    lam = (jnp.exp(jnp.sum(lq1_ref[...] * lk1_ref[...], axis=-1, keepdims=True))
           - jnp.exp(jnp.sum(lq2_ref[...] * lk2_ref[...], axis=-1, keepdims=True)) + lam_init)
    o = (o12[:, :tq] - lam * o12[:, tq:]).T
    o_ref[0] = (_rms(o, g_ref[...]) * (1.0 - lam_init)).astype(_bf16)


def _dattn(qd, kd, vdt, lq1, lk1, lq2, lk2, g, lam_init):
    b, s, _ = qd.shape
    tq = min(ATTN_TQ, s)
    small = lambda a: pl.BlockSpec(a.shape, lambda bi, h, i: (0, 0))
    return pl.pallas_call(
        functools.partial(_dattn_kernel, lam_init=lam_init),
        grid=(b, HEADS, s // tq),
        in_specs=[small(lq1), small(lk1), small(lq2), small(lk2), small(g),
                  pl.BlockSpec((1, tq, LANES), lambda bi, h, i: (bi, i, h)),
                  pl.BlockSpec((1, s, LANES), lambda bi, h, i: (bi, 0, h)),
                  pl.BlockSpec((1, LANES, s), lambda bi, h, i: (bi, h, 0))],
        out_specs=pl.BlockSpec((1, tq, LANES), lambda bi, h, i: (bi, i, h)),
        out_shape=jax.ShapeDtypeStruct((b, s, GROUP_W), _bf16),
        compiler_params=_params(("arbitrary",) * 3),
        name="dattn",
    )(lq1, lk1, lq2, lk2, g, qd, kd, vdt)


def _mattn_kernel(qn_ref, qr_ref, kn_ref, vt_ref, kpe_ref, o_ref):
    q = jnp.concatenate([qn_ref[0], qr_ref[0]], axis=-1)
    k = jnp.concatenate([kn_ref[0], kpe_ref[0]], axis=-1)
    o_ref[0] = _softmax_pv_t(k, q, vt_ref[0]).T.astype(_bf16)


def _mattn(qn, qr, kn, vmt, kpe):
    b, s, _ = qn.shape
    tq = min(2 * ATTN_TQ, s)
    return pl.pallas_call(
        _mattn_kernel,
        grid=(b, HEADS, s // tq),
        in_specs=[pl.BlockSpec((1, tq, LANES), lambda bi, h, i: (bi, i, h)),
                  pl.BlockSpec((1, tq, LANES), lambda bi, h, i: (bi, i, h)),
                  pl.BlockSpec((1, s, LANES), lambda bi, h, i: (bi, 0, h)),
                  pl.BlockSpec((1, LANES, s), lambda bi, h, i: (bi, h, 0)),
                  pl.BlockSpec((1, s, LANES), lambda bi, h, i: (bi, 0, 0))],
        out_specs=pl.BlockSpec((1, tq, LANES), lambda bi, h, i: (bi, i, h)),
        out_shape=jax.ShapeDtypeStruct((b, s, GROUP_W), _bf16),
        compiler_params=_params(("arbitrary",) * 3),
        name="mattn",
    )(qn, qr, kn, vmt, kpe)


def _post_kernel(x_ref, oa_ref, ob_ref, gt_ref, sc_ref, sh_ref, gmla_ref, gpost_ref, gpre_ref,
                 woa_ref, wob_ref, wrh_ref, wrl_ref, x1_ref, h2_ref, aff_ref):
    obn = _rms(ob_ref[0].astype(_f32), gmla_ref[...]).astype(_bf16)
    m = _dot(oa_ref[0], woa_ref[...]) + _dot(obn, wob_ref[...])
    x1 = x_ref[0] + gt_ref[0] * _rms(m, gpost_ref[...])
    x1_ref[0] = x1
    h2 = _rms(x1, gpre_ref[...]) * (1.0 + sc_ref[0]) + sh_ref[0]
    hbits = pltpu.bitcast(h2.astype(_bf16).astype(_f32), jnp.uint32)
    half = hbits.shape[1] // 2
    h2_ref[0] = hbits[:, :half] | lax.shift_right_logical(hbits[:, half:], jnp.uint32(16))
    hh, hl = _split(h2)
    logits = _dot(hh, wrh_ref[...]) + _dot(hl, wrh_ref[...]) + _dot(hh, wrl_ref[...])
    lane = lax.broadcasted_iota(jnp.int32, logits.shape, 1)
    logits = jnp.where(lane < N_EXPERTS, logits, -jnp.inf)
    p = jnp.exp(logits - jnp.max(logits, axis=-1, keepdims=True))
    aff = p / jnp.sum(p, axis=-1, keepdims=True)
    aff_ref[...] = aff.T[0:N_EXPERTS, :]


def _post(x, oa, ob, gt, sc, sh, gmla, gpost, gpre, woa, wob, wrh, wrl):
    b, s, d = x.shape
    tm = min(256, s)
    nt = s // tm
    row = lambda width: pl.BlockSpec((1, tm, width), lambda bi, i: (bi, i, 0))
    mod = pl.BlockSpec((1, 1, d), lambda bi, i: (bi, 0, 0))
    return pl.pallas_call(
        _post_kernel,
        grid=(b, nt),
        in_specs=[row(d), row(GROUP_W), row(GROUP_W), mod, mod, mod, _const_spec(gmla.shape),
                  _const_spec(gpost.shape), _const_spec(gpre.shape), _const_spec(woa.shape),
                  _const_spec(wob.shape), _const_spec(wrh.shape), _const_spec(wrl.shape)],
        out_specs=[row(d), row(d // 2), pl.BlockSpec((N_EXPERTS, tm), lambda bi, i: (0, bi * nt + i))],
        out_shape=[jax.ShapeDtypeStruct((b, s, d), _f32), jax.ShapeDtypeStruct((b, s, d // 2), jnp.uint32),
                   jax.ShapeDtypeStruct((N_EXPERTS, b * s), _f32)],
        compiler_params=_params(("arbitrary", "arbitrary")),
        name="post",
    )(x, oa, ob, gt, sc, sh, gmla, gpost, gpre, woa, wob, wrh, wrl)


def _route_kernel(aff_ref, idx_ref, gate_ref, selpos_ref, bex_ref, gt_scr, eq_scr, need_scr, *, cap, n_tok):
    nb = ROUTE_BLOCKS
    bits = pltpu.bitcast(aff_ref[...], jnp.int32)

    def count(mask):
        c = jnp.sum(jnp.where(mask, 1.0, 0.0), axis=2, keepdims=True)
        return jnp.sum(c, axis=1, keepdims=True)

    def search(i, thr):
        cand = thr | jnp.left_shift(jnp.int32(1), 30 - i)
        return jnp.where(count(bits >= cand) >= cap, cand, thr)

    thr = lax.fori_loop(0, 31, search, jnp.zeros((N_EXPERTS, 1, 1), jnp.int32))
    gt = bits > thr
    gt_scr[...] = jnp.where(gt, 1.0, 0.0)
    eq_scr[...] = jnp.where(bits == thr, 1.0, 0.0)
    need_scr[...] = jnp.broadcast_to(cap - count(gt), need_scr.shape)

    r_io = lax.broadcasted_iota(jnp.int32, (nb, LANES), 0)
    c_io = lax.broadcasted_iota(jnp.int32, (nb, LANES), 1)
    triu = jnp.where(r_io <= c_io, 1.0, 0.0).astype(_bf16)
    tril = jnp.where(c_io <= r_io, 1.0, 0.0).astype(_bf16)
    reps = cap // LANES
    slot = lax.broadcasted_iota(jnp.int32, (nb, cap), 1).astype(_f32)
    row = lax.broadcasted_iota(jnp.int32, (nb, cap), 0).astype(_f32)

    def prefix(m):
        loc = _dot(m.astype(_bf16), triu)
        tot = jnp.broadcast_to(loc[:, LANES - 1:LANES], loc.shape)
        return loc, tot, _dot(tril, tot.astype(_bf16))

    def per_expert(e, carry):
        gte = gt_scr[e]
        eqe = eq_scr[e]
        need = need_scr[e][0:1, :]
        eloc, etot, ebinc = prefix(eqe)
        eq_before = ebinc - etot + eloc - eqe
        sel = jnp.maximum(gte, eqe * jnp.where(eq_before < need, 1.0, 0.0))
        loc, tot, binc = prefix(sel)
        bexc = binc - tot
        pos = bexc + loc - sel
        selpos_ref[e] = jnp.where(sel > 0.0, pos, -1.0).astype(jnp.int32)
        bex_ref[e] = bexc.T[0:8, :]

        binc_t = jnp.tile(binc, (1, reps))
        bexc_t = jnp.tile(bexc, (1, reps))
        blk = jnp.sum(jnp.where(binc_t <= slot, 1.0, 0.0), axis=0, keepdims=True)
        hit = row == blk
        onehot = jnp.where(hit, 1.0, 0.0).astype(_bf16)
        rank = slot[0:1, :] - jnp.sum(jnp.where(hit, bexc_t, 0.0), axis=0, keepdims=True)
        loc_rows = _dot(loc.T.astype(_bf16), onehot)
        within = jnp.sum(jnp.where(loc_rows <= rank, 1.0, 0.0), axis=0, keepdims=True)
        token = jnp.minimum(blk * LANES + within, n_tok - 1.0)
        idx_ref[e] = jnp.broadcast_to(token, (8, cap)).astype(jnp.int32)

        at = aff_ref[e].T
        a0 = at.astype(_bf16)
        r1 = at - a0.astype(_f32)
        a1 = r1.astype(_bf16)
        a2 = (r1 - a1.astype(_f32)).astype(_bf16)
        aff_rows = _dot(a0, onehot) + _dot(a1, onehot) + _dot(a2, onehot)
        gate = jnp.sum(jnp.where(row == within, aff_rows, 0.0), axis=0, keepdims=True)
        gate_ref[e] = jnp.broadcast_to(gate, (8, cap))
        return carry

    lax.fori_loop(0, N_EXPERTS, per_expert, 0)


def _route(aff3, cap, n_tok):
    nb = ROUTE_BLOCKS
    full = lambda shape: pl.BlockSpec(shape, lambda: (0,) * len(shape))
    return pl.pallas_call(
        functools.partial(_route_kernel, cap=cap, n_tok=n_tok),
        in_specs=[full((N_EXPERTS, nb, LANES))],
        out_specs=[full((N_EXPERTS, 8, cap)), full((N_EXPERTS, 8, cap)), full((N_EXPERTS, nb, LANES)),
                   full((N_EXPERTS, 8, LANES))],
        out_shape=[jax.ShapeDtypeStruct((N_EXPERTS, 8, cap), jnp.int32),
                   jax.ShapeDtypeStruct((N_EXPERTS, 8, cap), _f32),
                   jax.ShapeDtypeStruct((N_EXPERTS, nb, LANES), jnp.int32),
                   jax.ShapeDtypeStruct((N_EXPERTS, 8, LANES), _f32)],
        scratch_shapes=[pltpu.VMEM((N_EXPERTS, nb, LANES), _f32), pltpu.VMEM((N_EXPERTS, nb, LANES), _f32),
                        pltpu.VMEM((N_EXPERTS, 8, LANES), _f32)],
        compiler_params=pltpu.CompilerParams(vmem_limit_bytes=VMEM_LIMIT),
        name="route",
    )(aff3)


def _ffn_kernel(idx0_ref, idxn_ref, gate_ref, h_hbm, wg_ref, wu_ref, wd_ref, o_ref, rows_scr, xe_scr, acc_scr,
                sems, *, tc, n_tiles, nf):
    f = pl.program_id(2)
    g = pl.program_id(0) * pl.num_programs(1) + pl.program_id(1)
    slot = g & 1
    half = rows_scr.shape[2]
    per_step = tc // nf

    def row_copy(idx_ref, r, dst_slot):
        t = idx_ref[0, 0, r]
        return pltpu.make_async_copy(h_hbm.at[pl.ds(t, 1), :], rows_scr.at[dst_slot, pl.ds(r, 1), :],
                                     sems.at[dst_slot])

    @pl.when((g == 0) & (f == 0))
    def _():
        def issue(r, carry):
            row_copy(idx0_ref, r, 0).start()
            return carry

        lax.fori_loop(0, tc, issue, 0)

    @pl.when(f == 0)
    def _():
        pltpu.make_async_copy(h_hbm.at[pl.ds(0, tc), :], rows_scr.at[slot], sems.at[slot]).wait()
        u = rows_scr[slot]
        hi = pltpu.bitcast(u & jnp.uint32(0xFFFF0000), _f32)
        lo = pltpu.bitcast(lax.shift_left(u, jnp.uint32(16)), _f32)
        xe_scr[:, :half] = hi.astype(_bf16)
        xe_scr[:, half:] = lo.astype(_bf16)
        acc_scr[...] = jnp.zeros_like(acc_scr)

    xe = xe_scr[...]
    hg = _dot(xe, wg_ref[...].astype(_bf16))
    hu = _dot(xe, wu_ref[...].astype(_bf16))
    a = (hg * (1.0 / (1.0 + jnp.exp(-hg))) * hu).astype(_bf16)
    acc_scr[...] += _dot(a, wd_ref[...].astype(_bf16))

    @pl.when(g + 1 < n_tiles)
    def _():
        for r in range(per_step):
            row_copy(idxn_ref, f * per_step + r, 1 - slot).start()

    @pl.when(f == nf - 1)
    def _():
        o_ref[0] = (acc_scr[...] * gate_ref[0]).astype(_bf16)


def _ffn(idx3, gate3, h2p, w_gate, w_up, w_down, cap):
    n, half = h2p.shape
    d = 2 * half
    tc = min(1024, cap)
    nc = cap // tc
    tf = min(256, D_FF)
    n_tiles = N_EXPERTS * nc
    idx_spec = lambda off: pl.BlockSpec(
        (1, 1, tc), lambda e, i, f: (jnp.minimum(e * nc + i + off, n_tiles - 1), 0, 0), memory_space=pltpu.SMEM)
    return pl.pallas_call(
        functools.partial(_ffn_kernel, tc=tc, n_tiles=n_tiles, nf=D_FF // tf),
        grid=(N_EXPERTS, nc, D_FF // tf),
        in_specs=[idx_spec(0), idx_spec(1),
                  pl.BlockSpec((1, tc, 1), lambda e, i, f: (e, i, 0)),
                  pl.BlockSpec(memory_space=pl.ANY),
                  pl.BlockSpec((None, d, tf), lambda e, i, f: (e, 0, f)),
                  pl.BlockSpec((None, d, tf), lambda e, i, f: (e, 0, f)),
                  pl.BlockSpec((None, tf, d), lambda e, i, f: (e, f, 0))],
        out_specs=pl.BlockSpec((1, tc, d), lambda e, i, f: (e, i, 0)),
        out_shape=jax.ShapeDtypeStruct((N_EXPERTS, cap, d), _bf16),
        scratch_shapes=[pltpu.VMEM((2, tc, half), jnp.uint32), pltpu.VMEM((tc, d), _bf16),
                        pltpu.VMEM((tc, d), _f32), pltpu.SemaphoreType.DMA((2,))],
        compiler_params=_params(("arbitrary",) * 3),
        name="ffn",
    )(idx3, idx3, gate3, h2p, w_gate, w_up, w_down)


_WIN = LANES
_ALIGN = 16
_CHUNK = _WIN - _ALIGN


def _combine_kernel(base_ref, basen_ref, cnt_ref, pos_ref, x1_ref, gt_ref, g_ref, ye_hbm, o_ref, stage, xstage,
                    y_scr, sems, xsem, *, cap):
    tt = pos_ref.shape[0]
    d = stage.shape[3]
    t = pl.program_id(0) * pl.num_programs(1) + pl.program_id(1)
    n_t = pl.num_programs(0) * pl.num_programs(1)
    slot = t & 1
    r_io = lax.broadcasted_iota(jnp.int32, (tt, _WIN), 1)

    def window(lo):
        aligned = lax.shift_left(lax.shift_right_logical(lo, 4), 4)
        return pl.multiple_of(jnp.minimum(aligned, cap - _WIN), _ALIGN)

    def copy(e, ws, dst_slot):
        return pltpu.make_async_copy(ye_hbm.at[e, pl.ds(ws, _WIN), :], stage.at[dst_slot, e], sems.at[dst_slot])

    def onehot(e, ws, lo):
        p = jnp.broadcast_to(pos_ref[:, e:e + 1], (tt, _WIN))
        hit = ((p - ws) == r_io) & (p >= lo) & (p < lo + _CHUNK)
        return jnp.where(hit, 1.0, 0.0).astype(_bf16)

    @pl.when(t == 0)
    def _():
        for e in range(N_EXPERTS):
            copy(e, window(base_ref[0, 0, e]), 0).start()

    @pl.when(t + 1 < n_t)
    def _():
        for e in range(N_EXPERTS):
            copy(e, window(basen_ref[0, 0, e]), 1 - slot).start()

    los = [base_ref[0, 0, e] for e in range(N_EXPERTS)]
    oh = jnp.concatenate([onehot(e, window(lo), lo) for e, lo in enumerate(los)], axis=1)
    for e, lo in enumerate(los):
        copy(e, window(lo), slot).wait()
    y_scr[...] = _dot(oh, stage[slot].reshape(N_EXPERTS * _WIN, d))

    for e in range(N_EXPERTS):
        n_chunks = (cnt_ref[0, 0, e] + _CHUNK - 1) // _CHUNK

        def extra(kc, carry, e=e):
            lo = los[e] + kc * _CHUNK
            ws = window(lo)
            cp = pltpu.make_async_copy(ye_hbm.at[e, pl.ds(ws, _WIN), :], xstage, xsem)
            cp.start()
            cp.wait()
            y_scr[...] += _dot(onehot(e, ws, lo), xstage[...])
            return carry

        lax.fori_loop(1, n_chunks, extra, 0)

    o_ref[0] = x1_ref[0] + gt_ref[0] * _rms(y_scr[...], g_ref[...])


def _combine(base, cnt, pos, x1, gt, g, ye, cap):
    b, s, d = x1.shape
    tt = min(512, s)
    nt = s // tt
    n_t = b * nt
    mod = pl.BlockSpec((1, 1, d), lambda bi, i: (bi, 0, 0))
    smem = lambda off: pl.BlockSpec(
        (1, 1, N_EXPERTS), lambda bi, i: (jnp.minimum(bi * nt + i + off, n_t - 1), 0, 0), memory_space=pltpu.SMEM)
    return pl.pallas_call(
        functools.partial(_combine_kernel, cap=cap),
        grid=(b, nt),
        in_specs=[smem(0), smem(1), smem(0),
                  pl.BlockSpec((tt, N_EXPERTS), lambda bi, i: (bi * nt + i, 0)),
                  pl.BlockSpec((1, tt, d), lambda bi, i: (bi, i, 0)), mod, _const_spec(g.shape),
                  pl.BlockSpec(memory_space=pl.ANY)],
        out_specs=pl.BlockSpec((1, tt, d), lambda bi, i: (bi, i, 0)),
        out_shape=jax.ShapeDtypeStruct((b, s, d), _f32),
        scratch_shapes=[pltpu.VMEM((2, N_EXPERTS, _WIN, d), _bf16), pltpu.VMEM((_WIN, d), _bf16),
                        pltpu.VMEM((tt, d), _f32), pltpu.SemaphoreType.DMA((2,)), pltpu.SemaphoreType.DMA(())],
        compiler_params=_params(("arbitrary", "arbitrary")),
        name="combine",
    )(base, base, cnt, pos, x1, gt, g, ye)


def _pair_perm():
    blk = jnp.concatenate([jnp.arange(0, 32), jnp.arange(64, 96), jnp.arange(32, 64), jnp.arange(96, 128)])
    return (jnp.arange(HEADS)[:, None] * LANES + blk[None, :]).reshape(-1)


def _pad_rope_cols(w):
    k = w.shape[0]
    w = w.reshape(k, -1, 2, 32)
    w = jnp.concatenate([w, jnp.zeros_like(w)], axis=-1)
    return w.reshape(k, -1)


def _prep_weights(w_in, w_uq, w_ukv, w_out, w_router):
    perm = _pair_perm()
    c = GROUP_W
    w1 = jnp.concatenate([
        w_in[:, 0:c][:, perm], w_in[:, c:2 * c][:, perm], w_in[:, 2 * c:3 * c],
        w_in[:, 3 * c:3 * c + Q_LORA], w_in[:, 3 * c + Q_LORA:3 * c + Q_LORA + KV_LORA],
        _pad_rope_cols(w_in[:, 3 * c + Q_LORA + KV_LORA:])], axis=1).astype(_bf16)
    wq = w_uq.reshape(Q_LORA, HEADS, MLA_QK)
    wqn = wq[:, :, :HEAD_W].reshape(Q_LORA, GROUP_W).astype(_bf16)
    wqr = _pad_rope_cols(wq[:, :, HEAD_W:].reshape(Q_LORA, HEADS * MLA_ROPE)).astype(_bf16)
    wkv = w_ukv.reshape(KV_LORA, HEADS, 2, HEAD_W).transpose(0, 2, 1, 3).reshape(KV_LORA, 2 * GROUP_W)
    wkv = wkv.astype(_bf16)
    woa = w_out[:GROUP_W].astype(_bf16)
    wob = w_out[GROUP_W:].astype(_bf16)
    wr = jnp.pad(w_router, ((0, 0), (0, LANES - N_EXPERTS)))
    wrh, wrl = _split(wr)
    return w1, wqn, wqr, wkv, woa, wob, wrh, wrl


def _rope_tables(s):
    half = DIFF_QK // 2
    inv = ROPE_THETA ** (-jnp.arange(half, dtype=_f32) / half)
    ang = jnp.arange(s, dtype=_f32)[:, None] * inv[None, :]
    cos = jnp.cos(ang)
    sin = jnp.sin(ang)
    return jnp.tile(cos, (1, 4)), jnp.concatenate([-sin, -sin, sin, sin], axis=1)


def _layer(x, mod, wts, lam_init):
    (g_pre_attn, g_post_attn, g_pre_ffn, g_post_ffn, lq1, lk1, lq2, lk2, g_diff_sub, g_q_lat, g_kv_lat,
     g_mla_out, w1, wqn, wqr, wkv, woa, wob, wrh, wrl, w_gate, w_up, w_down) = wts
    b, s, d = x.shape
    n = b * s
    cap = CAPACITY_FACTOR * n // N_EXPERTS
    sh_a, sc_a, gt_a, sh_f, sc_f, gt_f = [mod[:, None, i * d:(i + 1) * d] for i in range(6)]
    cos, sin = _rope_tables(s)

    qd, kd, vdt, qn, qr, kn, vmt, kpe = _proj(x, sc_a, sh_a, g_pre_attn, w1, g_q_lat, g_kv_lat, wqn, wqr, wkv,
                                              cos, sin)
    oa = _dattn(qd, kd, vdt, lq1, lk1, lq2, lk2, g_diff_sub, lam_init)
    ob = _mattn(qn, qr, kn, vmt, kpe)
    x1, h2, aff_t = _post(x, oa, ob, gt_a, sc_f, sh_f, g_mla_out, g_post_attn, g_pre_ffn, woa, wob, wrh, wrl)

    n_pad = ROUTE_BLOCKS * LANES
    aff3 = jnp.pad(aff_t, ((0, 0), (0, n_pad - n))).reshape(N_EXPERTS, ROUTE_BLOCKS, LANES)
    idx8, gate8, selpos, bex8 = _route(aff3, cap, n)

    tc = min(1024, cap)
    idx3 = idx8[:, 0, :].reshape(N_EXPERTS * (cap // tc), 1, tc)
    gate3 = gate8[:, 0, :].reshape(N_EXPERTS, cap, 1)
    ye = _ffn(idx3, gate3, h2.reshape(n, d // 2), w_gate, w_up, w_down, cap)

    tt = min(512, s)
    n_tiles = n // tt
    bex = bex8[:, 0, :].astype(jnp.int32).T
    base = bex[::tt // LANES][:n_tiles]
    cnt = jnp.concatenate([base[1:], jnp.full((1, N_EXPERTS), cap, jnp.int32)], axis=0) - base
    pos = selpos.reshape(N_EXPERTS, n_pad).T[:n]
    return _combine(base.reshape(n_tiles, 1, N_EXPERTS), cnt.reshape(n_tiles, 1, N_EXPERTS), pos, x1, gt_f,
                    g_post_ffn, ye, cap)


def kernel(x_prompt, x_sample, c_prompt, c_sample, w_ada, b_ada, g_pre_attn, g_post_attn, g_pre_ffn, g_post_ffn, w_in, lam_q1, lam_k1, lam_q2, lam_k2, g_diff_sub, g_q_lat, w_uq, g_kv_lat, w_ukv, g_mla_out, w_out, w_router, w_gate, w_up, w_down):
    y_prompt, y_sample = x_prompt, x_sample
    bp = c_prompt.shape[0]
    bs = c_sample.shape[0]
    rows = -(-(bp + bs) // 8) * 8
    c_all = jnp.concatenate([c_prompt, c_sample, jnp.zeros((rows - bp - bs, c_prompt.shape[1]), _f32)], axis=0)
    for l in range(DEPTH):
        lam_init = 0.8 - 0.6 * math.exp(-0.3 * l)
        mod = _ada(c_all, w_ada[l], b_ada[l][None, :])
        row = lambda a: a[l][None, :]
        wts = (row(g_pre_attn), row(g_post_attn), row(g_pre_ffn), row(g_post_ffn), row(lam_q1), row(lam_k1),
               row(lam_q2), row(lam_k2), row(g_diff_sub), row(g_q_lat), row(g_kv_lat), row(g_mla_out),
               *_prep_weights(w_in[l], w_uq[l], w_ukv[l], w_out[l], w_router[l]),
               w_gate[l], w_up[l], w_down[l])
        y_prompt = _layer(y_prompt, mod[:bp], wts, lam_init)
        y_sample = _layer(y_sample, mod[bp:bp + bs], wts, lam_init)
    return (y_prompt, y_sample)
```

```python
import functools
import math

import jax
import jax.numpy as jnp
from jax import lax
from jax.experimental import pallas as pl
from jax.experimental.pallas import tpu as pltpu

D_MODEL = 2048
DEPTH = 1
ROPE_THETA = 10000.0
NORM_EPS = 1e-6

HEADS = D_MODEL // 256
DIFF_QK = 64
HEAD_W = 128
MLA_ROPE = 64
MLA_QK = HEAD_W + MLA_ROPE
Q_LORA = D_MODEL // 4
KV_LORA = D_MODEL // 8
GROUP_W = HEADS * HEAD_W
N_EXPERTS = 16
CAPACITY_FACTOR = 2
D_FF = D_MODEL

LANES = 128
ROUTE_BLOCKS = 128
VMEM_LIMIT = 56 * 1024 * 1024
ATTN_TQ = 512
ATTN_KEY_CHUNK = 1024

_O_QD = 0
_O_KD = _O_QD + GROUP_W
_O_VD = _O_KD + GROUP_W
_O_MQ = _O_VD + GROUP_W
_O_MKV = _O_MQ + Q_LORA
_O_KR = _O_MKV + KV_LORA
_W1_COLS = _O_KR + LANES

_f32 = jnp.float32
_bf16 = jnp.bfloat16
_LOG2E = math.log2(math.e)


def _dot(a, b):
    return jnp.dot(a, b, preferred_element_type=_f32)


def _dot_nt(a, b):
    return lax.dot_general(a, b, (((1,), (1,)), ((), ())), preferred_element_type=_f32)


def _split(x):
    hi = x.astype(_bf16)
    lo = (x - hi.astype(_f32)).astype(_bf16)
    return hi, lo


def _dot3(a, b):
    ah, al = _split(a)
    bh, bl = _split(b)
    return _dot(ah, bh) + _dot(al, bh) + _dot(ah, bl)


def _rms(x, g):
    return x * lax.rsqrt(jnp.mean(x * x, axis=-1, keepdims=True) + NORM_EPS) * g


def _rope(x, cos, sin):
    lane = lax.broadcasted_iota(jnp.int32, x.shape, 1)
    partner = jnp.where((lane & 63) < 32, pltpu.roll(x, LANES - 32, 1), pltpu.roll(x, 32, 1))
    return x * cos + partner * sin


def _params(sem, vmem=VMEM_LIMIT):
    return pltpu.CompilerParams(dimension_semantics=sem, vmem_limit_bytes=vmem)


def _const_spec(shape):
    nd = len(shape)
    return pl.BlockSpec(shape, lambda *_: (0,) * nd, pipeline_mode=pl.Buffered(1))


def _ada_kernel(c_ref, w_ref, b_ref, o_ref):
    c = c_ref[...]
    s = c * (1.0 / (1.0 + jnp.exp(-c)))
    o_ref[...] = _dot3(s, w_ref[...]) + b_ref[...]


def _ada(c, w, b):
    rows, d = c.shape
    cols = w.shape[1]
    tn = 1024
    return pl.pallas_call(
        _ada_kernel,
        grid=(cols // tn,),
        in_specs=[pl.BlockSpec((rows, d), lambda j: (0, 0)),
                  pl.BlockSpec((d, tn), lambda j: (0, j)),
                  pl.BlockSpec((1, tn), lambda j: (0, j))],
        out_specs=pl.BlockSpec((rows, tn), lambda j: (0, j)),
        out_shape=jax.ShapeDtypeStruct((rows, cols), _f32),
        compiler_params=_params(("arbitrary",)),
        name="ada",
    )(c, w, b)


def _proj_kernel(x_ref, sc_ref, sh_ref, g_ref, w1_ref, gq_ref, gkv_ref, wqn_ref, wqr_ref, wkv_ref,
                 cos_ref, sin_ref, qd_ref, kd_ref, vdt_ref, qn_ref, qr_ref, kn_ref, vmt_ref, kpe_ref):
    x = x_ref[0]
    h = _rms(x, g_ref[...]) * (1.0 + sc_ref[0]) + sh_ref[0]
    hb = h.astype(_bf16)
    cos = cos_ref[...]
    sin = sin_ref[...]

    q = _dot(hb, w1_ref[:, _O_QD:_O_QD + GROUP_W])
    for hd in range(HEADS):
        sl = slice(hd * LANES, (hd + 1) * LANES)
        qd_ref[0, :, sl] = (_rope(q[:, sl], cos, sin) * (DIFF_QK ** -0.5 * _LOG2E)).astype(_bf16)
    k = _dot(hb, w1_ref[:, _O_KD:_O_KD + GROUP_W])
    for hd in range(HEADS):
        sl = slice(hd * LANES, (hd + 1) * LANES)
        kd_ref[0, :, sl] = _rope(k[:, sl], cos, sin).astype(_bf16)
    vdt_ref[0] = _dot(hb, w1_ref[:, _O_VD:_O_VD + GROUP_W]).T.astype(_bf16)

    ql = _rms(_dot(hb, w1_ref[:, _O_MQ:_O_MQ + Q_LORA]), gq_ref[...]).astype(_bf16)
    mscale = MLA_QK ** -0.5 * _LOG2E
    qn_ref[0] = (_dot(ql, wqn_ref[...]) * mscale).astype(_bf16)
    qr = _dot(ql, wqr_ref[...])
    for hd in range(HEADS):
        sl = slice(hd * LANES, (hd + 1) * LANES)
        qr_ref[0, :, sl] = (_rope(qr[:, sl], cos, sin) * mscale).astype(_bf16)

    kvl = _rms(_dot(hb, w1_ref[:, _O_MKV:_O_MKV + KV_LORA]), gkv_ref[...]).astype(_bf16)
    kv = _dot(kvl, wkv_ref[...])
    kn_ref[0] = kv[:, :GROUP_W].astype(_bf16)
    vmt_ref[0] = kv[:, GROUP_W:].T.astype(_bf16)
    kr = _dot(hb, w1_ref[:, _O_KR:_O_KR + MLA_ROPE])
    kr = jnp.concatenate([kr, jnp.zeros_like(kr)], axis=1)
    kpe_ref[0] = _rope(kr, cos, sin).astype(_bf16)


def _proj(x, sc, sh, g, w1, gq, gkv, wqn, wqr, wkv, cos, sin):
    b, s, d = x.shape
    tm = min(512, s)
    row = lambda width: pl.BlockSpec((1, tm, width), lambda bi, i: (bi, i, 0))
    mod = pl.BlockSpec((1, 1, d), lambda bi, i: (bi, 0, 0))
    tab = pl.BlockSpec((tm, LANES), lambda bi, i: (i, 0))
    col = pl.BlockSpec((1, GROUP_W, tm), lambda bi, i: (bi, 0, i))
    rows = jax.ShapeDtypeStruct((b, s, GROUP_W), _bf16)
    cols = jax.ShapeDtypeStruct((b, GROUP_W, s), _bf16)
    return pl.pallas_call(
        _proj_kernel,
        grid=(b, s // tm),
        in_specs=[row(d), mod, mod, _const_spec(g.shape), _const_spec(w1.shape), _const_spec(gq.shape),
                  _const_spec(gkv.shape), _const_spec(wqn.shape), _const_spec(wqr.shape),
                  _const_spec(wkv.shape), tab, tab],
        out_specs=[row(GROUP_W), row(GROUP_W), col, row(GROUP_W), row(GROUP_W), row(GROUP_W), col, row(LANES)],
        out_shape=[rows, rows, cols, rows, rows, rows, cols, jax.ShapeDtypeStruct((b, s, LANES), _bf16)],
        compiler_params=_params(("arbitrary", "arbitrary")),
        name="proj",
    )(x, sc, sh, g, w1, gq, gkv, wqn, wqr, wkv, cos, sin)


def _softmax_pv_t(k, q, vt):
    s = k.shape[0]
    ck = min(ATTN_KEY_CHUNK, s)
    m = l = o = None
    n = s // ck
    scores = lambda c: _dot_nt(k[c * ck:(c + 1) * ck], q)
    st_next = scores(0)
    for c in range(n):
        st = st_next
        if c + 1 < n:
            st_next = scores(c + 1)
        mc = jnp.max(st, axis=0, keepdims=True)
        m_new = mc if c == 0 else jnp.maximum(m, mc)
        p = jnp.exp2(st - m_new)
        lc = jnp.sum(p, axis=0, keepdims=True)
        oc = _dot(vt[:, c * ck:(c + 1) * ck], p.astype(_bf16))
        if c == 0:
            l, o = lc, oc
        else:
            a = jnp.exp2(m - m_new)
            l = l * a + lc
            o = o * a + oc
        m = m_new
    return o * (1.0 / l)


def _dattn_kernel(lq1_ref, lk1_ref, lq2_ref, lk2_ref, g_ref, q_ref, k_ref, vt_ref, o_ref, *, lam_init):
    q = q_ref[0]
    k = k_ref[0]
    vt = vt_ref[0]
    lane = lax.broadcasted_iota(jnp.int32, q.shape, 1)
    first = lane < DIFF_QK
    zero = jnp.zeros_like(q)
    tq = q.shape[0]
    o12 = _softmax_pv_t(k, jnp.concatenate([jnp.where(first, q, zero), jnp.where(first, zero, q)], axis=0), vt)
    lam = (jnp.exp(jnp.sum(lq1_ref[...] * lk1_ref[...], axis=-1, keepdims=True))
           - jnp.exp(jnp.sum(lq2_ref[...] * lk2_ref[...], axis=-1, keepdims=True)) + lam_init)
    o = (o12[:, :tq] - lam * o12[:, tq:]).T
    o_ref[0] = (_rms(o, g_ref[...]) * (1.0 - lam_init)).astype(_bf16)


def _dattn(qd, kd, vdt, lq1, lk1, lq2, lk2, g, lam_init):
    b, s, _ = qd.shape
    tq = min(ATTN_TQ, s)
    small = lambda a: pl.BlockSpec(a.shape, lambda bi, h, i: (0, 0))
    return pl.pallas_call(
        functools.partial(_dattn_kernel, lam_init=lam_init),
        grid=(b, HEADS, s // tq),
        in_specs=[small(lq1), small(lk1), small(lq2), small(lk2), small(g),
                  pl.BlockSpec((1, tq, LANES), lambda bi, h, i: (bi, i, h)),
                  pl.BlockSpec((1, s, LANES), lambda bi, h, i: (bi, 0, h)),
                  pl.BlockSpec((1, LANES, s), lambda bi, h, i: (bi, h, 0))],
        out_specs=pl.BlockSpec((1, tq, LANES), lambda bi, h, i: (bi, i, h)),
        out_shape=jax.ShapeDtypeStruct((b, s, GROUP_W), _bf16),
        compiler_params=_params(("arbitrary",) * 3),
        name="dattn",
    )(lq1, lk1, lq2, lk2, g, qd, kd, vdt)


def _mattn_kernel(qn_ref, qr_ref, kn_ref, vt_ref, kpe_ref, o_ref):
    q = jnp.concatenate([qn_ref[0], qr_ref[0]], axis=-1)
    k = jnp.concatenate([kn_ref[0], kpe_ref[0]], axis=-1)
    o_ref[0] = _softmax_pv_t(k, q, vt_ref[0]).T.astype(_bf16)


def _mattn(qn, qr, kn, vmt, kpe):
    b, s, _ = qn.shape
    tq = min(2 * ATTN_TQ, s)
    return pl.pallas_call(
        _mattn_kernel,
        grid=(b, HEADS, s // tq),
        in_specs=[pl.BlockSpec((1, tq, LANES), lambda bi, h, i: (bi, i, h)),
                  pl.BlockSpec((1, tq, LANES), lambda bi, h, i: (bi, i, h)),
                  pl.BlockSpec((1, s, LANES), lambda bi, h, i: (bi, 0, h)),
                  pl.BlockSpec((1, LANES, s), lambda bi, h, i: (bi, h, 0)),
                  pl.BlockSpec((1, s, LANES), lambda bi, h, i: (bi, 0, 0))],
        out_specs=pl.BlockSpec((1, tq, LANES), lambda bi, h, i: (bi, i, h)),
        out_shape=jax.ShapeDtypeStruct((b, s, GROUP_W), _bf16),
        compiler_params=_params(("arbitrary",) * 3),
        name="mattn",
    )(qn, qr, kn, vmt, kpe)


def _post_kernel(x_ref, oa_ref, ob_ref, gt_ref, sc_ref, sh_ref, gmla_ref, gpost_ref, gpre_ref,
                 wo_ref, wr_ref, x1_ref, h2_ref, aff_ref):
    obn = _rms(ob_ref[0].astype(_f32), gmla_ref[...]).astype(_bf16)
    m = _dot(oa_ref[0], wo_ref[:GROUP_W, :]) + _dot(obn, wo_ref[GROUP_W:, :])
    x1 = x_ref[0] + gt_ref[0] * _rms(m, gpost_ref[...])
    x1_ref[0] = x1
    h2 = _rms(x1, gpre_ref[...]) * (1.0 + sc_ref[0]) + sh_ref[0]
    hbits = pltpu.bitcast(h2.astype(_bf16).astype(_f32), jnp.uint32)
    half = hbits.shape[1] // 2
    h2_ref[0] = hbits[:, :half] | lax.shift_right_logical(hbits[:, half:], jnp.uint32(16))
    hh, hl = _split(h2)
    parts = _dot(hh, wr_ref[...]) + _dot(hl, wr_ref[...])
    logits = parts + pltpu.roll(parts, LANES - N_EXPERTS, 1)
    lane = lax.broadcasted_iota(jnp.int32, logits.shape, 1)
    logits = jnp.where(lane < N_EXPERTS, logits, -jnp.inf)
    p = jnp.exp(logits - jnp.max(logits, axis=-1, keepdims=True))
    aff = p / jnp.sum(p, axis=-1, keepdims=True)
    aff_ref[...] = aff.T[0:N_EXPERTS, :]


def _post(x, oa, ob, gt, sc, sh, gmla, gpost, gpre, wo, wr):
    b, s, d = x.shape
    tm = min(256, s)
    nt = s // tm
    row = lambda width: pl.BlockSpec((1, tm, width), lambda bi, i: (bi, i, 0))
    mod = pl.BlockSpec((1, 1, d), lambda bi, i: (bi, 0, 0))
    return pl.pallas_call(
        _post_kernel,
        grid=(b, nt),
        in_specs=[row(d), row(GROUP_W), row(GROUP_W), mod, mod, mod, _const_spec(gmla.shape),
                  _const_spec(gpost.shape), _const_spec(gpre.shape), _const_spec(wo.shape),
                  _const_spec(wr.shape)],
        out_specs=[row(d), row(d // 2), pl.BlockSpec((N_EXPERTS, tm), lambda bi, i: (0, bi * nt + i))],
        out_shape=[jax.ShapeDtypeStruct((b, s, d), _f32), jax.ShapeDtypeStruct((b, s, d // 2), jnp.uint32),
                   jax.ShapeDtypeStruct((N_EXPERTS, b * s), _f32)],
        compiler_params=_params(("arbitrary", "arbitrary")),
        name="post",
    )(x, oa, ob, gt, sc, sh, gmla, gpost, gpre, wo, wr)


def _route_kernel(aff_ref, idx_ref, gate_ref, selpos_ref, bex_ref, gt_scr, eq_scr, need_scr, *, cap, n_tok):
    nb = ROUTE_BLOCKS
    bits = pltpu.bitcast(aff_ref[...], jnp.int32)

    def count(mask):
        c = jnp.sum(jnp.where(mask, 1.0, 0.0), axis=2, keepdims=True)
        return jnp.sum(c, axis=1, keepdims=True)

    def search(i, thr):
        cand = thr | jnp.left_shift(jnp.int32(1), 30 - i)
        return jnp.where(count(bits >= cand) >= cap, cand, thr)

    thr = lax.fori_loop(0, 31, search, jnp.zeros((N_EXPERTS, 1, 1), jnp.int32))
    gt = bits > thr
    gt_scr[...] = jnp.where(gt, 1.0, 0.0)
    eq_scr[...] = jnp.where(bits == thr, 1.0, 0.0)
    need_scr[...] = jnp.broadcast_to(cap - count(gt), need_scr.shape)

    r_io = lax.broadcasted_iota(jnp.int32, (nb, LANES), 0)
    c_io = lax.broadcasted_iota(jnp.int32, (nb, LANES), 1)
    triu = jnp.where(r_io <= c_io, 1.0, 0.0).astype(_bf16)
    tril = jnp.where(c_io <= r_io, 1.0, 0.0).astype(_bf16)
    reps = cap // LANES
    slot = lax.broadcasted_iota(jnp.int32, (nb, cap), 1).astype(_f32)
    row = lax.broadcasted_iota(jnp.int32, (nb, cap), 0).astype(_f32)

    def prefix(m):
        loc = _dot(m.astype(_bf16), triu)
        tot = jnp.broadcast_to(loc[:, LANES - 1:LANES], loc.shape)
        return loc, tot, _dot(tril, tot.astype(_bf16))

    def per_expert(e, carry):
        gte = gt_scr[e]
        eqe = eq_scr[e]
        need = need_scr[e][0:1, :]
        eloc, etot, ebinc = prefix(eqe)
        eq_before = ebinc - etot + eloc - eqe
        sel = jnp.maximum(gte, eqe * jnp.where(eq_before < need, 1.0, 0.0))
        loc, tot, binc = prefix(sel)
        bexc = binc - tot
        pos = bexc + loc - sel
        selpos_ref[e] = jnp.where(sel > 0.0, pos, -1.0).astype(jnp.int32)
        bex_ref[e] = bexc.T[0:8, :]

        binc_t = jnp.tile(binc, (1, reps))
        bexc_t = jnp.tile(bexc, (1, reps))
        blk = jnp.sum(jnp.where(binc_t <= slot, 1.0, 0.0), axis=0, keepdims=True)
        hit = row == blk
        onehot = jnp.where(hit, 1.0, 0.0).astype(_bf16)
        rank = slot[0:1, :] - jnp.sum(jnp.where(hit, bexc_t, 0.0), axis=0, keepdims=True)
        loc_rows = _dot(loc.T.astype(_bf16), onehot)
        within = jnp.sum(jnp.where(loc_rows <= rank, 1.0, 0.0), axis=0, keepdims=True)
        token = jnp.minimum(blk * LANES + within, n_tok - 1.0)
        idx_ref[e] = jnp.broadcast_to(token, (8, cap)).astype(jnp.int32)

        at = aff_ref[e].T
        a0 = at.astype(_bf16)
        r1 = at - a0.astype(_f32)
        a1 = r1.astype(_bf16)
        a2 = (r1 - a1.astype(_f32)).astype(_bf16)
        aff_rows = _dot(a0, onehot) + _dot(a1, onehot) + _dot(a2, onehot)
        gate = jnp.sum(jnp.where(row == within, aff_rows, 0.0), axis=0, keepdims=True)
        gate_ref[e] = jnp.broadcast_to(gate, (8, cap))
        return carry

    lax.fori_loop(0, N_EXPERTS, per_expert, 0)


def _route(aff3, cap, n_tok):
    nb = ROUTE_BLOCKS
    full = lambda shape: pl.BlockSpec(shape, lambda: (0,) * len(shape))
    return pl.pallas_call(
        functools.partial(_route_kernel, cap=cap, n_tok=n_tok),
        in_specs=[full((N_EXPERTS, nb, LANES))],
        out_specs=[full((N_EXPERTS, 8, cap)), full((N_EXPERTS, 8, cap)), full((N_EXPERTS, nb, LANES)),
                   full((N_EXPERTS, 8, LANES))],
        out_shape=[jax.ShapeDtypeStruct((N_EXPERTS, 8, cap), jnp.int32),
                   jax.ShapeDtypeStruct((N_EXPERTS, 8, cap), _f32),
                   jax.ShapeDtypeStruct((N_EXPERTS, nb, LANES), jnp.int32),
                   jax.ShapeDtypeStruct((N_EXPERTS, 8, LANES), _f32)],
        scratch_shapes=[pltpu.VMEM((N_EXPERTS, nb, LANES), _f32), pltpu.VMEM((N_EXPERTS, nb, LANES), _f32),
                        pltpu.VMEM((N_EXPERTS, 8, LANES), _f32)],
        compiler_params=pltpu.CompilerParams(vmem_limit_bytes=VMEM_LIMIT),
        name="route",
    )(aff3)


def _ffn_kernel(idx0_ref, idxn_ref, gate_ref, h_hbm, wg_ref, wu_ref, wd_ref, o_ref, rows_scr, xe_scr, acc_scr,
                sems, *, tc, n_tiles, nf):
    f = pl.program_id(2)
    g = pl.program_id(0) * pl.num_programs(1) + pl.program_id(1)
    slot = g & 1
    half = rows_scr.shape[2]
    per_step = tc // nf

    def row_copy(idx_ref, r, dst_slot):
        t = idx_ref[0, 0, r]
        return pltpu.make_async_copy(h_hbm.at[pl.ds(t, 1), :], rows_scr.at[dst_slot, pl.ds(r, 1), :],
                                     sems.at[dst_slot])

    @pl.when((g == 0) & (f == 0))
    def _():
        def issue(r, carry):
            row_copy(idx0_ref, r, 0).start()
            return carry

        lax.fori_loop(0, tc, issue, 0)

    @pl.when(f == 0)
    def _():
        pltpu.make_async_copy(h_hbm.at[pl.ds(0, tc), :], rows_scr.at[slot], sems.at[slot]).wait()
        u = rows_scr[slot]
        hi = pltpu.bitcast(u & jnp.uint32(0xFFFF0000), _f32)
        lo = pltpu.bitcast(lax.shift_left(u, jnp.uint32(16)), _f32)
        xe_scr[:, :half] = hi.astype(_bf16)
        xe_scr[:, half:] = lo.astype(_bf16)
        acc_scr[...] = jnp.zeros_like(acc_scr)

    xe = xe_scr[...]
    hg = _dot(xe, wg_ref[...].astype(_bf16))
    hu = _dot(xe, wu_ref[...].astype(_bf16))
    a = (hg * (1.0 / (1.0 + jnp.exp(-hg))) * hu).astype(_bf16)
    acc_scr[...] += _dot(a, wd_ref[...].astype(_bf16))

    @pl.when(g + 1 < n_tiles)
    def _():
        for r in range(per_step):
            row_copy(idxn_ref, f * per_step + r, 1 - slot).start()

    @pl.when(f == nf - 1)
    def _():
        gate_col = jnp.broadcast_to(gate_ref[0, 0:1, :], (LANES, tc)).T[:, 0:1]
        o_ref[0] = (acc_scr[...] * gate_col).astype(_bf16)


def _ffn(idx3, gate3, h2p, w_gate, w_up, w_down, cap):
    n, half = h2p.shape
    d = 2 * half
    tc = min(1024, cap)
    nc = cap // tc
    tf = min(256, D_FF)
    n_tiles = N_EXPERTS * nc
    idx_spec = lambda off: pl.BlockSpec(
        (1, 1, tc), lambda e, i, f: (jnp.minimum(e * nc + i + off, n_tiles - 1), 0, 0), memory_space=pltpu.SMEM)
    return pl.pallas_call(
        functools.partial(_ffn_kernel, tc=tc, n_tiles=n_tiles, nf=D_FF // tf),
        grid=(N_EXPERTS, nc, D_FF // tf),
        in_specs=[idx_spec(0), idx_spec(1),
                  pl.BlockSpec((1, 8, tc), lambda e, i, f: (e, 0, i)),
                  pl.BlockSpec(memory_space=pl.ANY),
                  pl.BlockSpec((None, d, tf), lambda e, i, f: (e, 0, f)),
                  pl.BlockSpec((None, d, tf), lambda e, i, f: (e, 0, f)),
                  pl.BlockSpec((None, tf, d), lambda e, i, f: (e, f, 0))],
        out_specs=pl.BlockSpec((1, tc, d), lambda e, i, f: (e, i, 0)),
        out_shape=jax.ShapeDtypeStruct((N_EXPERTS, cap, d), _bf16),
        scratch_shapes=[pltpu.VMEM((2, tc, half), jnp.uint32), pltpu.VMEM((tc, d), _bf16),
                        pltpu.VMEM((tc, d), _f32), pltpu.SemaphoreType.DMA((2,))],
        compiler_params=_params(("arbitrary",) * 3),
        name="ffn",
    )(idx3, idx3, gate3, h2p, w_gate, w_up, w_down)


_WIN = LANES
_ALIGN = 16
_CHUNK = _WIN - _ALIGN


def _combine_kernel(base_ref, basen_ref, cnt_ref, pos_ref, x1_ref, gt_ref, g_ref, ye_hbm, o_ref, stage, xstage,
                    y_scr, sems, xsem, *, cap):
    tt = pos_ref.shape[0]
    d = stage.shape[3]
    t = pl.program_id(0) * pl.num_programs(1) + pl.program_id(1)
    n_t = pl.num_programs(0) * pl.num_programs(1)
    slot = t & 1
    r_io = lax.broadcasted_iota(jnp.int32, (tt, _WIN), 1)

    def window(lo):
        aligned = lax.shift_left(lax.shift_right_logical(lo, 4), 4)
        return pl.multiple_of(jnp.minimum(aligned, cap - _WIN), _ALIGN)

    def copy(e, ws, dst_slot):
        return pltpu.make_async_copy(ye_hbm.at[e, pl.ds(ws, _WIN), :], stage.at[dst_slot, e], sems.at[dst_slot])

    def onehot(e, ws, lo):
        want = jnp.where((r_io[0:1] >= lo - ws) & (r_io[0:1] < lo - ws + _CHUNK), r_io[0:1] + ws, -2)
        p = jnp.broadcast_to(pos_ref[:, e:e + 1], (tt, _WIN))
        return jnp.where(p == want, 1.0, 0.0).astype(_bf16)

    @pl.when(t == 0)
    def _():
        for e in range(N_EXPERTS):
            copy(e, window(base_ref[0, 0, e]), 0).start()

    @pl.when(t + 1 < n_t)
    def _():
        for e in range(N_EXPERTS):
            copy(e, window(basen_ref[0, 0, e]), 1 - slot).start()

    los = [base_ref[0, 0, e] for e in range(N_EXPERTS)]
    oh = jnp.concatenate([onehot(e, window(lo), lo) for e, lo in enumerate(los)], axis=1)
    for e, lo in enumerate(los):
        copy(e, window(lo), slot).wait()
    y_scr[...] = _dot(oh, stage[slot].reshape(N_EXPERTS * _WIN, d))

    for e in range(N_EXPERTS):
        cnt = cnt_ref[0, 0, e]
        n_chunks = sum((cnt > kc * _CHUNK).astype(jnp.int32) for kc in range(-(-tt // _CHUNK)))

        def extra(kc, carry, e=e):
            lo = los[e] + kc * _CHUNK
            ws = window(lo)
            cp = pltpu.make_async_copy(ye_hbm.at[e, pl.ds(ws, _WIN), :], xstage, xsem)
            cp.start()
            cp.wait()
            y_scr[...] += _dot(onehot(e, ws, lo), xstage[...])
            return carry

        lax.fori_loop(1, n_chunks, extra, 0)

    o_ref[0] = x1_ref[0] + gt_ref[0] * _rms(y_scr[...], g_ref[...])


def _combine(base, cnt, pos, x1, gt, g, ye, cap):
    b, s, d = x1.shape
    tt = min(512, s)
    nt = s // tt
    n_t = b * nt
    mod = pl.BlockSpec((1, 1, d), lambda bi, i: (bi, 0, 0))
    smem = lambda off: pl.BlockSpec(
        (1, 1, N_EXPERTS), lambda bi, i: (jnp.minimum(bi * nt + i + off, n_t - 1), 0, 0), memory_space=pltpu.SMEM)
    return pl.pallas_call(
        functools.partial(_combine_kernel, cap=cap),
        grid=(b, nt),
        in_specs=[smem(0), smem(1), smem(0),
                  pl.BlockSpec((tt, N_EXPERTS), lambda bi, i: (bi * nt + i, 0)),
                  pl.BlockSpec((1, tt, d), lambda bi, i: (bi, i, 0)), mod, _const_spec(g.shape),
                  pl.BlockSpec(memory_space=pl.ANY)],
        out_specs=pl.BlockSpec((1, tt, d), lambda bi, i: (bi, i, 0)),
        out_shape=jax.ShapeDtypeStruct((b, s, d), _f32),
        scratch_shapes=[pltpu.VMEM((2, N_EXPERTS, _WIN, d), _bf16), pltpu.VMEM((_WIN, d), _bf16),
                        pltpu.VMEM((tt, d), _f32), pltpu.SemaphoreType.DMA((2,)), pltpu.SemaphoreType.DMA(())],
        compiler_params=_params(("arbitrary", "arbitrary")),
        name="combine",
    )(base, base, cnt, pos, x1, gt, g, ye)


def _prep_weights(w_in, w_uq, w_ukv, w_out, w_router):
    w1 = w_in.astype(_bf16)
    wq = w_uq.astype(_bf16).reshape(Q_LORA, HEADS, MLA_QK)
    wqn = wq[:, :, :HEAD_W].reshape(Q_LORA, GROUP_W)
    wqr = jnp.pad(wq[:, :, HEAD_W:], ((0, 0), (0, 0), (0, LANES - MLA_ROPE))).reshape(Q_LORA, HEADS * LANES)
    wkv = w_ukv.astype(_bf16).reshape(KV_LORA, HEADS, 2, HEAD_W).transpose(0, 2, 1, 3)
    wkv = wkv.reshape(KV_LORA, 2 * GROUP_W)
    wo = w_out.astype(_bf16)
    wrh, wrl = _split(w_router)
    wr2 = jnp.pad(jnp.concatenate([wrh, wrl], axis=1), ((0, 0), (0, LANES - 2 * N_EXPERTS)))
    return w1, wqn, wqr, wkv, wo, wr2


def _rope_tables(s):
    half = DIFF_QK // 2
    inv = ROPE_THETA ** (-jnp.arange(half, dtype=_f32) / half)
    ang = jnp.arange(s, dtype=_f32)[:, None] * inv[None, :]
    cos = jnp.cos(ang)
    sin = jnp.sin(ang)
    return jnp.tile(cos, (1, 4)), jnp.concatenate([-sin, sin, -sin, sin], axis=1)


def _layer(x, mod, wts, lam_init):
    (g_pre_attn, g_post_attn, g_pre_ffn, g_post_ffn, lq1, lk1, lq2, lk2, g_diff_sub, g_q_lat, g_kv_lat,
     g_mla_out, w1, wqn, wqr, wkv, wo, wr, w_gate, w_up, w_down) = wts
    b, s, d = x.shape
    n = b * s
    cap = CAPACITY_FACTOR * n // N_EXPERTS
    sh_a, sc_a, gt_a, sh_f, sc_f, gt_f = [mod[:, None, i * d:(i + 1) * d] for i in range(6)]
    cos, sin = _rope_tables(s)

    qd, kd, vdt, qn, qr, kn, vmt, kpe = _proj(x, sc_a, sh_a, g_pre_attn, w1, g_q_lat, g_kv_lat, wqn, wqr, wkv,
                                              cos, sin)
    oa = _dattn(qd, kd, vdt, lq1, lk1, lq2, lk2, g_diff_sub, lam_init)
    ob = _mattn(qn, qr, kn, vmt, kpe)
    x1, h2, aff_t = _post(x, oa, ob, gt_a, sc_f, sh_f, g_mla_out, g_post_attn, g_pre_ffn, wo, wr)

    n_pad = ROUTE_BLOCKS * LANES
    aff3 = jnp.pad(aff_t, ((0, 0), (0, n_pad - n))).reshape(N_EXPERTS, ROUTE_BLOCKS, LANES)
    idx8, gate8, selpos, bex8 = _route(aff3, cap, n)

    tc = min(1024, cap)
    idx3 = idx8[:, 0, :].reshape(N_EXPERTS * (cap // tc), 1, tc)
    ye = _ffn(idx3, gate8, h2.reshape(n, d // 2), w_gate, w_up, w_down, cap)

    tt = min(512, s)
    n_tiles = n // tt
    bex = bex8[:, 0, :].astype(jnp.int32).T
    base = bex[::tt // LANES][:n_tiles]
    cnt = jnp.concatenate([base[1:], jnp.full((1, N_EXPERTS), cap, jnp.int32)], axis=0) - base
    pos = selpos.reshape(N_EXPERTS, n_pad).T[:n]
    return _combine(base.reshape(n_tiles, 1, N_EXPERTS), cnt.reshape(n_tiles, 1, N_EXPERTS), pos, x1, gt_f,
                    g_post_ffn, ye, cap)


def kernel(x_prompt, x_sample, c_prompt, c_sample, w_ada, b_ada, g_pre_attn, g_post_attn, g_pre_ffn, g_post_ffn, w_in, lam_q1, lam_k1, lam_q2, lam_k2, g_diff_sub, g_q_lat, w_uq, g_kv_lat, w_ukv, g_mla_out, w_out, w_router, w_gate, w_up, w_down):
    y_prompt, y_sample = x_prompt, x_sample
    bp = c_prompt.shape[0]
    bs = c_sample.shape[0]
    rows = -(-(bp + bs) // 8) * 8
    c_all = jnp.concatenate([c_prompt, c_sample, jnp.zeros((rows - bp - bs, c_prompt.shape[1]), _f32)], axis=0)
    for l in range(DEPTH):
        lam_init = 0.8 - 0.6 * math.exp(-0.3 * l)
        mod = _ada(c_all, w_ada[l], b_ada[l][None, :])
        row = lambda a: a[l][None, :]
        wts = (row(g_pre_attn), row(g_post_attn), row(g_pre_ffn), row(g_post_ffn), row(lam_q1), row(lam_k1),
               row(lam_q2), row(lam_k2), row(g_diff_sub), row(g_q_lat), row(g_kv_lat), row(g_mla_out),
               *_prep_weights(w_in[l], w_uq[l], w_ukv[l], w_out[l], w_router[l]),
               w_gate[l], w_up[l], w_down[l])
        y_prompt = _layer(y_prompt, mod[:bp], wts, lam_init)
        y_sample = _layer(y_sample, mod[bp:bp + bs], wts, lam_init)
    return (y_prompt, y_sample)
```

```python
import functools
import math

import jax
import jax.numpy as jnp
from jax import lax
from jax.experimental import pallas as pl
from jax.experimental.pallas import tpu as pltpu

D_MODEL = 2048
DEPTH = 1
ROPE_THETA = 10000.0
NORM_EPS = 1e-6

HEADS = D_MODEL // 256
DIFF_QK = 64
HEAD_W = 128
MLA_ROPE = 64
MLA_QK = HEAD_W + MLA_ROPE
Q_LORA = D_MODEL // 4
KV_LORA = D_MODEL // 8
GROUP_W = HEADS * HEAD_W
N_EXPERTS = 16
CAPACITY_FACTOR = 2
D_FF = D_MODEL

LANES = 128
ROUTE_BLOCKS = 128
ROUTE_BISECTIONS = 40
VMEM_LIMIT = 56 * 1024 * 1024
ATTN_TQ = 512
ATTN_KEY_CHUNK = 1024
ATTN_HEADS_PER_STEP = 4

_O_QD = 0
_O_KD = _O_QD + GROUP_W
_O_VD = _O_KD + GROUP_W
_O_MQ = _O_VD + GROUP_W
_O_MKV = _O_MQ + Q_LORA
_O_KR = _O_MKV + KV_LORA
_W1_COLS = _O_KR + LANES

_f32 = jnp.float32
_bf16 = jnp.bfloat16
_LOG2E = math.log2(math.e)


def _dot(a, b):
    return jnp.dot(a, b, preferred_element_type=_f32)


def _dot_nt(a, b):
    return lax.dot_general(a, b, (((1,), (1,)), ((), ())), preferred_element_type=_f32)


def _split(x):
    hi = x.astype(_bf16)
    lo = (x - hi.astype(_f32)).astype(_bf16)
    return hi, lo


def _dot3(a, b):
    ah, al = _split(a)
    bh, bl = _split(b)
    return _dot(ah, bh) + _dot(al, bh) + _dot(ah, bl)


def _rms(x, g):
    return x * lax.rsqrt(jnp.mean(x * x, axis=-1, keepdims=True) + NORM_EPS) * g


def _rope(x, cos, sin):
    lane = lax.broadcasted_iota(jnp.int32, x.shape, 1)
    partner = jnp.where((lane & 63) < 32, pltpu.roll(x, LANES - 32, 1), pltpu.roll(x, 32, 1))
    return x * cos + partner * sin


def _params(sem, vmem=VMEM_LIMIT):
    return pltpu.CompilerParams(dimension_semantics=sem, vmem_limit_bytes=vmem)


def _const_spec(shape):
    nd = len(shape)
    return pl.BlockSpec(shape, lambda *_: (0,) * nd, pipeline_mode=pl.Buffered(1))


def _ada_kernel(c_ref, w_ref, b_ref, o_ref):
    c = c_ref[...]
    s = c * (1.0 / (1.0 + jnp.exp(-c)))
    o_ref[...] = _dot3(s, w_ref[...]) + b_ref[...]


def _ada(c, w, b):
    rows, d = c.shape
    cols = w.shape[1]
    tn = 1024
    return pl.pallas_call(
        _ada_kernel,
        grid=(cols // tn,),
        in_specs=[pl.BlockSpec((rows, d), lambda j: (0, 0)),
                  pl.BlockSpec((d, tn), lambda j: (0, j)),
                  pl.BlockSpec((1, tn), lambda j: (0, j))],
        out_specs=pl.BlockSpec((rows, tn), lambda j: (0, j)),
        out_shape=jax.ShapeDtypeStruct((rows, cols), _f32),
        compiler_params=_params(("arbitrary",)),
        name="ada",
    )(c, w, b)


def _proj_kernel(x_ref, sc_ref, sh_ref, g_ref, w1_ref, gq_ref, gkv_ref, wqn_ref, wqr_ref, wkv_ref,
                 cos_ref, sin_ref, qd_ref, kd_ref, vdt_ref, qn_ref, qr_ref, kn_ref, vmt_ref, kpe_ref):
    x = x_ref[0]
    h = _rms(x, g_ref[...]) * (1.0 + sc_ref[0]) + sh_ref[0]
    hb = h.astype(_bf16)
    cos = cos_ref[...]
    sin = sin_ref[...]

    q = _dot(hb, w1_ref[:, _O_QD:_O_QD + GROUP_W])
    for hd in range(HEADS):
        sl = slice(hd * LANES, (hd + 1) * LANES)
        qd_ref[0, :, sl] = (_rope(q[:, sl], cos, sin) * (DIFF_QK ** -0.5 * _LOG2E)).astype(_bf16)
    k = _dot(hb, w1_ref[:, _O_KD:_O_KD + GROUP_W])
    for hd in range(HEADS):
        sl = slice(hd * LANES, (hd + 1) * LANES)
        kd_ref[0, :, sl] = _rope(k[:, sl], cos, sin).astype(_bf16)
    vdt_ref[0] = _dot(hb, w1_ref[:, _O_VD:_O_VD + GROUP_W]).T.astype(_bf16)

    ql = _rms(_dot(hb, w1_ref[:, _O_MQ:_O_MQ + Q_LORA]), gq_ref[...]).astype(_bf16)
    mscale = MLA_QK ** -0.5 * _LOG2E
    qn_ref[0] = (_dot(ql, wqn_ref[...]) * mscale).astype(_bf16)
    qr = _dot(ql, wqr_ref[...])
    for hd in range(HEADS):
        sl = slice(hd * LANES, (hd + 1) * LANES)
        qr_ref[0, :, sl] = (_rope(qr[:, sl], cos, sin) * mscale).astype(_bf16)

    kvl = _rms(_dot(hb, w1_ref[:, _O_MKV:_O_MKV + KV_LORA]), gkv_ref[...]).astype(_bf16)
    kv = _dot(kvl, wkv_ref[...])
    kn_ref[0] = kv[:, :GROUP_W].astype(_bf16)
    vmt_ref[0] = kv[:, GROUP_W:].T.astype(_bf16)
    kr = _dot(hb, w1_ref[:, _O_KR:_O_KR + MLA_ROPE])
    kr = jnp.concatenate([kr, jnp.zeros_like(kr)], axis=1)
    kpe_ref[0] = _rope(kr, cos, sin).astype(_bf16)


def _proj(x, sc, sh, g, w1, gq, gkv, wqn, wqr, wkv, cos, sin):
    b, s, d = x.shape
    tm = min(512, s)
    row = lambda width: pl.BlockSpec((1, tm, width), lambda bi, i: (bi, i, 0))
    mod = pl.BlockSpec((1, 1, d), lambda bi, i: (bi, 0, 0))
    tab = pl.BlockSpec((tm, LANES), lambda bi, i: (i, 0))
    col = pl.BlockSpec((1, GROUP_W, tm), lambda bi, i: (bi, 0, i))
    rows = jax.ShapeDtypeStruct((b, s, GROUP_W), _bf16)
    cols = jax.ShapeDtypeStruct((b, GROUP_W, s), _bf16)
    return pl.pallas_call(
        _proj_kernel,
        grid=(b, s // tm),
        in_specs=[row(d), mod, mod, _const_spec(g.shape), _const_spec(w1.shape), _const_spec(gq.shape),
                  _const_spec(gkv.shape), _const_spec(wqn.shape), _const_spec(wqr.shape),
                  _const_spec(wkv.shape), tab, tab],
        out_specs=[row(GROUP_W), row(GROUP_W), col, row(GROUP_W), row(GROUP_W), row(GROUP_W), col, row(LANES)],
        out_shape=[rows, rows, cols, rows, rows, rows, cols, jax.ShapeDtypeStruct((b, s, LANES), _bf16)],
        compiler_params=_params(("arbitrary", "arbitrary")),
        name="proj",
    )(x, sc, sh, g, w1, gq, gkv, wqn, wqr, wkv, cos, sin)


def _attend_t(k, q, vt, finish):
    s = k.shape[0]
    ck = min(ATTN_KEY_CHUNK, s)
    m = l = o = None
    n = s // ck
    scores = lambda c: _dot_nt(k[c * ck:(c + 1) * ck], q)
    st_next = scores(0)
    yield
    for c in range(n):
        st = st_next
        if c + 1 < n:
            st_next = scores(c + 1)
            yield
        mc = jnp.max(st, axis=0, keepdims=True)
        m_new = mc if c == 0 else jnp.maximum(m, mc)
        p = jnp.exp2(st - m_new)
        lc = jnp.sum(p, axis=0, keepdims=True)
        oc = _dot(vt[:, c * ck:(c + 1) * ck], p.astype(_bf16))
        if c == 0:
            l, o = lc, oc
        else:
            a = jnp.exp2(m - m_new)
            l = l * a + lc
            o = o * a + oc
        m = m_new
        if c + 1 < n:
            yield
    finish(o * (1.0 / l))


def _run_staggered(streams):
    live = []
    pending = list(streams)
    while live or pending:
        if pending:
            live.append(pending.pop(0))
        for g in list(live):
            try:
                next(g)
            except StopIteration:
                live.remove(g)


def _dattn_kernel(lq1_ref, lk1_ref, lq2_ref, lk2_ref, g_ref, q_ref, k_ref, vt_ref, o_ref, *, lam_init):
    tq = q_ref.shape[1]
    lam = (jnp.exp(jnp.sum(lq1_ref[...] * lk1_ref[...], axis=-1, keepdims=True))
           - jnp.exp(jnp.sum(lq2_ref[...] * lk2_ref[...], axis=-1, keepdims=True)) + lam_init)
    lane = lax.broadcasted_iota(jnp.int32, (tq, LANES), 1)
    first = lane < DIFF_QK

    def head(hd):
        sl = slice(hd * LANES, (hd + 1) * LANES)
        q = q_ref[0, :, sl]
        zero = jnp.zeros_like(q)

        def finish(o12):
            o = (o12[:, :tq] - lam * o12[:, tq:]).T
            o_ref[0, :, sl] = (_rms(o, g_ref[...]) * (1.0 - lam_init)).astype(_bf16)

        q12 = jnp.concatenate([jnp.where(first, q, zero), jnp.where(first, zero, q)], axis=0)
        return _attend_t(k_ref[0, :, sl], q12, vt_ref[0, sl, :], finish)

    _run_staggered([head(hd) for hd in range(ATTN_HEADS_PER_STEP)])


def _dattn(qd, kd, vdt, lq1, lk1, lq2, lk2, g, lam_init):
    b, s, _ = qd.shape
    tq = min(ATTN_TQ, s)
    w = ATTN_HEADS_PER_STEP * LANES
    small = lambda a: pl.BlockSpec(a.shape, lambda bi, h, i: (0, 0))
    return pl.pallas_call(
        functools.partial(_dattn_kernel, lam_init=lam_init),
        grid=(b, HEADS // ATTN_HEADS_PER_STEP, s // tq),
        in_specs=[small(lq1), small(lk1), small(lq2), small(lk2), small(g),
                  pl.BlockSpec((1, tq, w), lambda bi, h, i: (bi, i, h)),
                  pl.BlockSpec((1, s, w), lambda bi, h, i: (bi, 0, h)),
                  pl.BlockSpec((1, w, s), lambda bi, h, i: (bi, h, 0))],
        out_specs=pl.BlockSpec((1, tq, w), lambda bi, h, i: (bi, i, h)),
        out_shape=jax.ShapeDtypeStruct((b, s, GROUP_W), _bf16),
        compiler_params=_params(("arbitrary",) * 3),
        name="dattn",
    )(lq1, lk1, lq2, lk2, g, qd, kd, vdt)


def _mattn_kernel(qn_ref, qr_ref, kn_ref, vt_ref, kpe_ref, o_ref):
    kpe = kpe_ref[0]

    def head(hd):
        sl = slice(hd * LANES, (hd + 1) * LANES)

        def finish(o):
            o_ref[0, :, sl] = o.T.astype(_bf16)

        q = jnp.concatenate([qn_ref[0, :, sl], qr_ref[0, :, sl]], axis=-1)
        k = jnp.concatenate([kn_ref[0, :, sl], kpe], axis=-1)
        return _attend_t(k, q, vt_ref[0, sl, :], finish)

    _run_staggered([head(hd) for hd in range(ATTN_HEADS_PER_STEP)])


def _mattn(qn, qr, kn, vmt, kpe):
    b, s, _ = qn.shape
    tq = min(2 * ATTN_TQ, s)
    w = ATTN_HEADS_PER_STEP * LANES
    return pl.pallas_call(
        _mattn_kernel,
        grid=(b, HEADS // ATTN_HEADS_PER_STEP, s // tq),
        in_specs=[pl.BlockSpec((1, tq, w), lambda bi, h, i: (bi, i, h)),
                  pl.BlockSpec((1, tq, w), lambda bi, h, i: (bi, i, h)),
                  pl.BlockSpec((1, s, w), lambda bi, h, i: (bi, 0, h)),
                  pl.BlockSpec((1, w, s), lambda bi, h, i: (bi, h, 0)),
                  pl.BlockSpec((1, s, LANES), lambda bi, h, i: (bi, 0, 0))],
        out_specs=pl.BlockSpec((1, tq, w), lambda bi, h, i: (bi, i, h)),
        out_shape=jax.ShapeDtypeStruct((b, s, GROUP_W), _bf16),
        compiler_params=_params(("arbitrary",) * 3),
        name="mattn",
    )(qn, qr, kn, vmt, kpe)


def _post_kernel(x_ref, oa_ref, ob_ref, gt_ref, sc_ref, sh_ref, gmla_ref, gpost_ref, gpre_ref,
                 wo_ref, wr_ref, x1_ref, h2_ref, aff_ref):
    obn = _rms(ob_ref[0].astype(_f32), gmla_ref[...]).astype(_bf16)
    m = _dot(oa_ref[0], wo_ref[:GROUP_W, :]) + _dot(obn, wo_ref[GROUP_W:, :])
    x1 = x_ref[0] + gt_ref[0] * _rms(m, gpost_ref[...])
    x1_ref[0] = x1
    h2 = _rms(x1, gpre_ref[...]) * (1.0 + sc_ref[0]) + sh_ref[0]
    h2_ref[0] = h2
    hh, hl = _split(h2)
    parts = _dot(hh, wr_ref[...]) + _dot(hl, wr_ref[...])
    logits = parts + pltpu.roll(parts, LANES - N_EXPERTS, 1)
    lane = lax.broadcasted_iota(jnp.int32, logits.shape, 1)
    logits = jnp.where(lane < N_EXPERTS, logits, -jnp.inf)
    p = jnp.exp(logits - jnp.max(logits, axis=-1, keepdims=True))
    aff = p / jnp.sum(p, axis=-1, keepdims=True)
    aff_ref[...] = aff.T[0:N_EXPERTS, :]


def _post(x, oa, ob, gt, sc, sh, gmla, gpost, gpre, wo, wr):
    b, s, d = x.shape
    tm = min(256, s)
    nt = s // tm
    row = lambda width: pl.BlockSpec((1, tm, width), lambda bi, i: (bi, i, 0))
    mod = pl.BlockSpec((1, 1, d), lambda bi, i: (bi, 0, 0))
    return pl.pallas_call(
        _post_kernel,
        grid=(b, nt),
        in_specs=[row(d), row(GROUP_W), row(GROUP_W), mod, mod, mod, _const_spec(gmla.shape),
                  _const_spec(gpost.shape), _const_spec(gpre.shape), _const_spec(wo.shape),
                  _const_spec(wr.shape)],
        out_specs=[row(d), row(d), pl.BlockSpec((N_EXPERTS, tm), lambda bi, i: (0, bi * nt + i))],
        out_shape=[jax.ShapeDtypeStruct((b, s, d), _f32), jax.ShapeDtypeStruct((b, s, d), _f32),
                   jax.ShapeDtypeStruct((N_EXPERTS, b * s), _f32)],
        compiler_params=_params(("arbitrary", "arbitrary")),
        name="post",
    )(x, oa, ob, gt, sc, sh, gmla, gpost, gpre, wo, wr)


def _route_kernel(aff_ref, idx_ref, gate_ref, selpos_ref, bex_ref, gt_scr, eq_scr, need_scr, *, cap, n_tok):
    nb = ROUTE_BLOCKS
    aff = aff_ref[...]

    def count_ge(t):
        c = jnp.sum(jnp.where(aff >= t, 1.0, 0.0), axis=2, keepdims=True)
        return jnp.sum(c, axis=1, keepdims=True)

    hi = jnp.full((N_EXPERTS, 1, 1), 2.0, _f32)
    for shift in (64, 32, 16, 8, 4, 2, 1):
        cand = hi * (2.0 ** -shift)
        hi = jnp.where(count_ge(cand) < cap, cand, hi)
    lo = jnp.where(count_ge(hi * 0.5) >= cap, hi * 0.5, 0.0)

    def bisect(i, bracket):
        lo, hi = bracket
        mid = 0.5 * (lo + hi)
        keep = count_ge(mid) >= cap
        return jnp.where(keep, mid, lo), jnp.where(keep, hi, mid)

    lo, hi = lax.fori_loop(0, ROUTE_BISECTIONS, bisect, (lo, hi))
    above = jnp.where(aff >= hi, 1.0, 0.0)
    gt_scr[...] = above
    eq_scr[...] = jnp.where(aff >= lo, 1.0, 0.0) - above
    need_scr[...] = jnp.broadcast_to(cap - count_ge(hi), need_scr.shape)

    r_io = lax.broadcasted_iota(jnp.int32, (nb, LANES), 0)
    c_io = lax.broadcasted_iota(jnp.int32, (nb, LANES), 1)
    triu = jnp.where(r_io <= c_io, 1.0, 0.0).astype(_bf16)
    tril = jnp.where(c_io <= r_io, 1.0, 0.0).astype(_bf16)
    reps = cap // LANES
    slot = lax.broadcasted_iota(jnp.int32, (nb, cap), 1).astype(_f32)
    row = lax.broadcasted_iota(jnp.int32, (nb, cap), 0).astype(_f32)

    def prefix(m):
        loc = _dot(m.astype(_bf16), triu)
        tot = jnp.broadcast_to(loc[:, LANES - 1:LANES], loc.shape)
        return loc, tot, _dot(tril, tot.astype(_bf16))

    def per_expert(e, carry):
        gte = gt_scr[e]
        eqe = eq_scr[e]
        need = need_scr[e][0:1, :]
        eloc, etot, ebinc = prefix(eqe)
        eq_before = ebinc - etot + eloc - eqe
        sel = jnp.maximum(gte, eqe * jnp.where(eq_before < need, 1.0, 0.0))
        loc, tot, binc = prefix(sel)
        bexc = binc - tot
        pos = bexc + loc - sel
        selpos_ref[e] = jnp.where(sel > 0.0, pos, -1.0).astype(jnp.int32)
        bex_ref[e] = bexc.T[0:8, :]

        binc_t = jnp.tile(binc, (1, reps))
        bexc_t = jnp.tile(bexc, (1, reps))
        blk = jnp.sum(jnp.where(binc_t <= slot, 1.0, 0.0), axis=0, keepdims=True)
        hit = row == blk
        onehot = jnp.where(hit, 1.0, 0.0).astype(_bf16)
        rank = slot[0:1, :] - jnp.sum(jnp.where(hit, bexc_t, 0.0), axis=0, keepdims=True)
        loc_rows = _dot(loc.T.astype(_bf16), onehot)
        within = jnp.sum(jnp.where(loc_rows <= rank, 1.0, 0.0), axis=0, keepdims=True)
        token = jnp.minimum(blk * LANES + within, n_tok - 1.0)
        idx_ref[e] = jnp.broadcast_to(token, (8, cap)).astype(jnp.int32)

        at = aff_ref[e].T
        a0 = at.astype(_bf16)
        r1 = at - a0.astype(_f32)
        a1 = r1.astype(_bf16)
        a2 = (r1 - a1.astype(_f32)).astype(_bf16)
        aff_rows = _dot(a0, onehot) + _dot(a1, onehot) + _dot(a2, onehot)
        gate = jnp.sum(jnp.where(row == within, aff_rows, 0.0), axis=0, keepdims=True)
        gate_ref[e] = jnp.broadcast_to(gate, (8, cap))
        return carry

    lax.fori_loop(0, N_EXPERTS, per_expert, 0)


def _route(aff3, cap, n_tok):
    nb = ROUTE_BLOCKS
    full = lambda shape: pl.BlockSpec(shape, lambda: (0,) * len(shape))
    return pl.pallas_call(
        functools.partial(_route_kernel, cap=cap, n_tok=n_tok),
        in_specs=[full((N_EXPERTS, nb, LANES))],
        out_specs=[full((N_EXPERTS, 8, cap)), full((N_EXPERTS, 8, cap)), full((N_EXPERTS, nb, LANES)),
                   full((N_EXPERTS, 8, LANES))],
        out_shape=[jax.ShapeDtypeStruct((N_EXPERTS, 8, cap), jnp.int32),
                   jax.ShapeDtypeStruct((N_EXPERTS, 8, cap), _f32),
                   jax.ShapeDtypeStruct((N_EXPERTS, nb, LANES), jnp.int32),
                   jax.ShapeDtypeStruct((N_EXPERTS, 8, LANES), _f32)],
        scratch_shapes=[pltpu.VMEM((N_EXPERTS, nb, LANES), _f32), pltpu.VMEM((N_EXPERTS, nb, LANES), _f32),
                        pltpu.VMEM((N_EXPERTS, 8, LANES), _f32)],
        compiler_params=pltpu.CompilerParams(vmem_limit_bytes=VMEM_LIMIT),
        name="route",
    )(aff3)


def _ffn_kernel(idx0_ref, idxn_ref, gate_ref, h_hbm, wg_ref, wu_ref, wd_ref, o_ref, rows_scr, xe_scr, acc_scr,
                sem, *, tc, n_tiles, nf):
    f = pl.program_id(2)
    g = pl.program_id(0) * pl.num_programs(1) + pl.program_id(1)
    per_step = tc // nf

    def row_copy(idx_ref, r):
        t = idx_ref[0, 0, r]
        return pltpu.make_async_copy(h_hbm.at[pl.ds(t, 1), :], rows_scr.at[pl.ds(r, 1), :], sem)

    @pl.when((g == 0) & (f == 0))
    def _():
        def issue(r, carry):
            row_copy(idx0_ref, r).start()
            return carry

        lax.fori_loop(0, tc, issue, 0)

    @pl.when(f == 0)
    def _():
        pltpu.make_async_copy(h_hbm.at[pl.ds(0, tc), :], rows_scr, sem).wait()
        xe_scr[...] = rows_scr[...].astype(_bf16)
        acc_scr[...] = jnp.zeros_like(acc_scr)

    xe = xe_scr[...]
    hg = _dot(xe, wg_ref[...].astype(_bf16))
    hu = _dot(xe, wu_ref[...].astype(_bf16))
    a = (hg * (1.0 / (1.0 + jnp.exp(-hg))) * hu).astype(_bf16)
    acc_scr[...] += _dot(a, wd_ref[...].astype(_bf16))

    @pl.when(g + 1 < n_tiles)
    def _():
        for r in range(per_step):
            row_copy(idxn_ref, f * per_step + r).start()

    @pl.when(f == nf - 1)
    def _():
        gate_col = jnp.broadcast_to(gate_ref[0, 0:1, :], (LANES, tc)).T[:, 0:1]
        o_ref[0] = (acc_scr[...] * gate_col).astype(_bf16)


def _ffn(idx3, gate3, h2, w_gate, w_up, w_down, cap):
    n, d = h2.shape
    tc = min(1024, cap)
    nc = cap // tc
    tf = min(256, D_FF)
    n_tiles = N_EXPERTS * nc
    idx_spec = lambda off: pl.BlockSpec(
        (1, 1, tc), lambda e, i, f: (jnp.minimum(e * nc + i + off, n_tiles - 1), 0, 0), memory_space=pltpu.SMEM)
    return pl.pallas_call(
        functools.partial(_ffn_kernel, tc=tc, n_tiles=n_tiles, nf=D_FF // tf),
        grid=(N_EXPERTS, nc, D_FF // tf),
        in_specs=[idx_spec(0), idx_spec(1),
                  pl.BlockSpec((1, 8, tc), lambda e, i, f: (e, 0, i)),
                  pl.BlockSpec(memory_space=pl.ANY),
                  pl.BlockSpec((None, d, tf), lambda e, i, f: (e, 0, f)),
                  pl.BlockSpec((None, d, tf), lambda e, i, f: (e, 0, f)),
                  pl.BlockSpec((None, tf, d), lambda e, i, f: (e, f, 0))],
        out_specs=pl.BlockSpec((1, tc, d), lambda e, i, f: (e, i, 0)),
        out_shape=jax.ShapeDtypeStruct((N_EXPERTS, cap, d), _bf16),
        scratch_shapes=[pltpu.VMEM((tc, d), _f32), pltpu.VMEM((tc, d), _bf16),
                        pltpu.VMEM((tc, d), _f32), pltpu.SemaphoreType.DMA(())],
        compiler_params=_params(("arbitrary",) * 3),
        name="ffn",
    )(idx3, idx3, gate3, h2, w_gate, w_up, w_down)


_WIN = LANES
_ALIGN = 16
_CHUNK = _WIN - _ALIGN


def _combine_kernel(base_ref, basen_ref, cnt_ref, pos_ref, x1_ref, gt_ref, g_ref, ye_hbm, o_ref, stage, xstage,
                    y_scr, sems, xsem, *, cap):
    tt = pos_ref.shape[0]
    d = stage.shape[3]
    t = pl.program_id(0) * pl.num_programs(1) + pl.program_id(1)
    n_t = pl.num_programs(0) * pl.num_programs(1)
    slot = t & 1
    r_io = lax.broadcasted_iota(jnp.int32, (tt, _WIN), 1)

    def window(lo):
        aligned = lax.shift_left(lax.shift_right_logical(lo, 4), 4)
        return pl.multiple_of(jnp.minimum(aligned, cap - _WIN), _ALIGN)

    def copy(e, ws, dst_slot):
        return pltpu.make_async_copy(ye_hbm.at[e, pl.ds(ws, _WIN), :], stage.at[dst_slot, e], sems.at[dst_slot])

    def onehot(e, ws, lo):
        want = jnp.where((r_io[0:1] >= lo - ws) & (r_io[0:1] < lo - ws + _CHUNK), r_io[0:1] + ws, -2)
        p = jnp.broadcast_to(pos_ref[:, e:e + 1], (tt, _WIN))
        return jnp.where(p == want, 1.0, 0.0).astype(_bf16)

    @pl.when(t == 0)
    def _():
        for e in range(N_EXPERTS):
            copy(e, window(base_ref[0, 0, e]), 0).start()

    @pl.when(t + 1 < n_t)
    def _():
        for e in range(N_EXPERTS):
            copy(e, window(basen_ref[0, 0, e]), 1 - slot).start()

    los = [base_ref[0, 0, e] for e in range(N_EXPERTS)]
    oh = jnp.concatenate([onehot(e, window(lo), lo) for e, lo in enumerate(los)], axis=1)
    for e, lo in enumerate(los):
        copy(e, window(lo), slot).wait()
    y_scr[...] = _dot(oh, stage[slot].reshape(N_EXPERTS * _WIN, d))

    for e in range(N_EXPERTS):
        cnt = cnt_ref[0, 0, e]
        n_chunks = sum((cnt > kc * _CHUNK).astype(jnp.int32) for kc in range(-(-tt // _CHUNK)))

        def extra(kc, carry, e=e):
            lo = los[e] + kc * _CHUNK
            ws = window(lo)
            cp = pltpu.make_async_copy(ye_hbm.at[e, pl.ds(ws, _WIN), :], xstage, xsem)
            cp.start()
            cp.wait()
            y_scr[...] += _dot(onehot(e, ws, lo), xstage[...])
            return carry

        lax.fori_loop(1, n_chunks, extra, 0)

    o_ref[0] = x1_ref[0] + gt_ref[0] * _rms(y_scr[...], g_ref[...])


def _combine(base, cnt, pos, x1, gt, g, ye, cap):
    b, s, d = x1.shape
    tt = min(512, s)
    nt = s // tt
    n_t = b * nt
    mod = pl.BlockSpec((1, 1, d), lambda bi, i: (bi, 0, 0))
    smem = lambda off: pl.BlockSpec(
        (1, 1, N_EXPERTS), lambda bi, i: (jnp.minimum(bi * nt + i + off, n_t - 1), 0, 0), memory_space=pltpu.SMEM)
    return pl.pallas_call(
        functools.partial(_combine_kernel, cap=cap),
        grid=(b, nt),
        in_specs=[smem(0), smem(1), smem(0),
                  pl.BlockSpec((tt, N_EXPERTS), lambda bi, i: (bi * nt + i, 0)),
                  pl.BlockSpec((1, tt, d), lambda bi, i: (bi, i, 0)), mod, _const_spec(g.shape),
                  pl.BlockSpec(memory_space=pl.ANY)],
        out_specs=pl.BlockSpec((1, tt, d), lambda bi, i: (bi, i, 0)),
        out_shape=jax.ShapeDtypeStruct((b, s, d), _f32),
        scratch_shapes=[pltpu.VMEM((2, N_EXPERTS, _WIN, d), _bf16), pltpu.VMEM((_WIN, d), _bf16),
                        pltpu.VMEM((tt, d), _f32), pltpu.SemaphoreType.DMA((2,)), pltpu.SemaphoreType.DMA(())],
        compiler_params=_params(("arbitrary", "arbitrary")),
        name="combine",
    )(base, base, cnt, pos, x1, gt, g, ye)


def _prep_weights(w_in, w_uq, w_ukv, w_out, w_router):
    w1 = w_in.astype(_bf16)
    wq = w_uq.astype(_bf16).reshape(Q_LORA, HEADS, MLA_QK)
    wqn = wq[:, :, :HEAD_W].reshape(Q_LORA, GROUP_W)
    wqr = jnp.pad(wq[:, :, HEAD_W:], ((0, 0), (0, 0), (0, LANES - MLA_ROPE))).reshape(Q_LORA, HEADS * LANES)
    wkv = w_ukv.astype(_bf16).reshape(KV_LORA, HEADS, 2, HEAD_W).transpose(0, 2, 1, 3)
    wkv = wkv.reshape(KV_LORA, 2 * GROUP_W)
    wo = w_out.astype(_bf16)
    wrh, wrl = _split(w_router)
    wr2 = jnp.pad(jnp.concatenate([wrh, wrl], axis=1), ((0, 0), (0, LANES - 2 * N_EXPERTS)))
    return w1, wqn, wqr, wkv, wo, wr2


def _rope_tables(s):
    half = DIFF_QK // 2
    inv = ROPE_THETA ** (-jnp.arange(half, dtype=_f32) / half)
    ang = jnp.arange(s, dtype=_f32)[:, None] * inv[None, :]
    cos = jnp.cos(ang)
    sin = jnp.sin(ang)
    return jnp.tile(cos, (1, 4)), jnp.concatenate([-sin, sin, -sin, sin], axis=1)


def _layer(x, mod, wts, lam_init):
    (g_pre_attn, g_post_attn, g_pre_ffn, g_post_ffn, lq1, lk1, lq2, lk2, g_diff_sub, g_q_lat, g_kv_lat,
     g_mla_out, w1, wqn, wqr, wkv, wo, wr, w_gate, w_up, w_down) = wts
    b, s, d = x.shape
    n = b * s
    cap = CAPACITY_FACTOR * n // N_EXPERTS
    sh_a, sc_a, gt_a, sh_f, sc_f, gt_f = [mod[:, None, i * d:(i + 1) * d] for i in range(6)]
    cos, sin = _rope_tables(s)

    qd, kd, vdt, qn, qr, kn, vmt, kpe = _proj(x, sc_a, sh_a, g_pre_attn, w1, g_q_lat, g_kv_lat, wqn, wqr, wkv,
                                              cos, sin)
    oa = _dattn(qd, kd, vdt, lq1, lk1, lq2, lk2, g_diff_sub, lam_init)
    ob = _mattn(qn, qr, kn, vmt, kpe)
    x1, h2, aff_t = _post(x, oa, ob, gt_a, sc_f, sh_f, g_mla_out, g_post_attn, g_pre_ffn, wo, wr)

    n_pad = ROUTE_BLOCKS * LANES
    aff3 = jnp.pad(aff_t, ((0, 0), (0, n_pad - n))).reshape(N_EXPERTS, ROUTE_BLOCKS, LANES)
    idx8, gate8, selpos, bex8 = _route(aff3, cap, n)

    tc = min(1024, cap)
    idx3 = idx8[:, 0, :].reshape(N_EXPERTS * (cap // tc), 1, tc)
    ye = _ffn(idx3, gate8, h2.reshape(n, d), w_gate, w_up, w_down, cap)

    tt = min(512, s)
    n_tiles = n // tt
    bex = bex8[:, 0, :].astype(jnp.int32).T
    base = bex[::tt // LANES][:n_tiles]
    cnt = jnp.concatenate([base[1:], jnp.full((1, N_EXPERTS), cap, jnp.int32)], axis=0) - base
    pos = selpos.reshape(N_EXPERTS, n_pad).T[:n]
    return _combine(base.reshape(n_tiles, 1, N_EXPERTS), cnt.reshape(n_tiles, 1, N_EXPERTS), pos, x1, gt_f,
                    g_post_ffn, ye, cap)


def kernel(x_prompt, x_sample, c_prompt, c_sample, w_ada, b_ada, g_pre_attn, g_post_attn, g_pre_ffn, g_post_ffn, w_in, lam_q1, lam_k1, lam_q2, lam_k2, g_diff_sub, g_q_lat, w_uq, g_kv_lat, w_ukv, g_mla_out, w_out, w_router, w_gate, w_up, w_down):
    y_prompt, y_sample = x_prompt, x_sample
    bp = c_prompt.shape[0]
    bs = c_sample.shape[0]
    rows = -(-(bp + bs) // 8) * 8
    c_all = jnp.concatenate([c_prompt, c_sample, jnp.zeros((rows - bp - bs, c_prompt.shape[1]), _f32)], axis=0)
    for l in range(DEPTH):
        lam_init = 0.8 - 0.6 * math.exp(-0.3 * l)
        mod = _ada(c_all, w_ada[l], b_ada[l][None, :])
        row = lambda a: a[l][None, :]
        wts = (row(g_pre_attn), row(g_post_attn), row(g_pre_ffn), row(g_post_ffn), row(lam_q1), row(lam_k1),
               row(lam_q2), row(lam_k2), row(g_diff_sub), row(g_q_lat), row(g_kv_lat), row(g_mla_out),
               *_prep_weights(w_in[l], w_uq[l], w_ukv[l], w_out[l], w_router[l]),
               w_gate[l], w_up[l], w_down[l])
        y_prompt = _layer(y_prompt, mod[:bp], wts, lam_init)
        y_sample = _layer(y_sample, mod[bp:bp + bs], wts, lam_init)
    return (y_prompt, y_sample)
```

```python
import functools
import math

import jax
import jax.numpy as jnp
from jax import lax
from jax.experimental import pallas as pl
from jax.experimental.pallas import tpu as pltpu

D_MODEL = 2048
DEPTH = 1
ROPE_THETA = 10000.0
NORM_EPS = 1e-6

HEADS = D_MODEL // 256
DIFF_QK = 64
HEAD_W = 128
MLA_ROPE = 64
MLA_QK = HEAD_W + MLA_ROPE
Q_LORA = D_MODEL // 4
KV_LORA = D_MODEL // 8
GROUP_W = HEADS * HEAD_W
N_EXPERTS = 16
CAPACITY_FACTOR = 2
D_FF = D_MODEL

LANES = 128
ROUTE_BLOCKS = 128
ROUTE_BISECTIONS = 40
VMEM_LIMIT = 56 * 1024 * 1024
ATTN_TQ = 512
ATTN_KEY_CHUNK = 1024
POST_SUB = 256
POST_STREAMS = 2
ATTN_HEADS_PER_STEP = 4

_O_QD = 0
_O_KD = _O_QD + GROUP_W
_O_VD = _O_KD + GROUP_W
_O_MQ = _O_VD + GROUP_W
_O_MKV = _O_MQ + Q_LORA
_O_KR = _O_MKV + KV_LORA
_W1_COLS = _O_KR + LANES

_f32 = jnp.float32
_bf16 = jnp.bfloat16
_LOG2E = math.log2(math.e)


def _dot(a, b):
    return jnp.dot(a, b, preferred_element_type=_f32)


def _dot_nt(a, b):
    return lax.dot_general(a, b, (((1,), (1,)), ((), ())), preferred_element_type=_f32)


def _split(x):
    hi = x.astype(_bf16)
    lo = (x - hi.astype(_f32)).astype(_bf16)
    return hi, lo


def _dot3(a, b):
    ah, al = _split(a)
    bh, bl = _split(b)
    return _dot(ah, bh) + _dot(al, bh) + _dot(ah, bl)


def _rms(x, g):
    return x * lax.rsqrt(jnp.mean(x * x, axis=-1, keepdims=True) + NORM_EPS) * g


def _rope(x, cos, sin):
    lane = lax.broadcasted_iota(jnp.int32, x.shape, 1)
    partner = jnp.where((lane & 63) < 32, pltpu.roll(x, LANES - 32, 1), pltpu.roll(x, 32, 1))
    return x * cos + partner * sin


def _params(sem, vmem=VMEM_LIMIT):
    return pltpu.CompilerParams(dimension_semantics=sem, vmem_limit_bytes=vmem)


def _const_spec(shape):
    nd = len(shape)
    return pl.BlockSpec(shape, lambda *_: (0,) * nd, pipeline_mode=pl.Buffered(1))


def _ada_kernel(c_ref, w_ref, b_ref, o_ref):
    c = c_ref[...]
    s = c * (1.0 / (1.0 + jnp.exp(-c)))
    o_ref[...] = _dot3(s, w_ref[...]) + b_ref[...]


def _ada(c, w, b):
    rows, d = c.shape
    cols = w.shape[1]
    tn = 1024
    return pl.pallas_call(
        _ada_kernel,
        grid=(cols // tn,),
        in_specs=[pl.BlockSpec((rows, d), lambda j: (0, 0)),
                  pl.BlockSpec((d, tn), lambda j: (0, j)),
                  pl.BlockSpec((1, tn), lambda j: (0, j))],
        out_specs=pl.BlockSpec((rows, tn), lambda j: (0, j)),
        out_shape=jax.ShapeDtypeStruct((rows, cols), _f32),
        compiler_params=_params(("arbitrary",)),
        name="ada",
    )(c, w, b)


def _proj_kernel(x_ref, sc_ref, sh_ref, g_ref, w1_ref, gq_ref, gkv_ref, wqn_ref, wqr_ref, wkv_ref,
                 cos_ref, sin_ref, qd_ref, kd_ref, vdt_ref, qn_ref, qr_ref, kn_ref, vmt_ref, kpe_ref):
    x = x_ref[0]
    h = _rms(x, g_ref[...]) * (1.0 + sc_ref[0]) + sh_ref[0]
    hb = h.astype(_bf16)
    cos = cos_ref[...]
    sin = sin_ref[...]

    ql = _rms(_dot(hb, w1_ref[:, _O_MQ:_O_MQ + Q_LORA]), gq_ref[...]).astype(_bf16)
    kvl = _rms(_dot(hb, w1_ref[:, _O_MKV:_O_MKV + KV_LORA]), gkv_ref[...]).astype(_bf16)
    kr = _dot(hb, w1_ref[:, _O_KR:_O_KR + MLA_ROPE])
    kr = jnp.concatenate([kr, jnp.zeros_like(kr)], axis=1)
    kpe_ref[0] = _rope(kr, cos, sin).astype(_bf16)

    q = _dot(hb, w1_ref[:, _O_QD:_O_QD + GROUP_W])
    for hd in range(HEADS):
        sl = slice(hd * LANES, (hd + 1) * LANES)
        qd_ref[0, :, sl] = (_rope(q[:, sl], cos, sin) * (DIFF_QK ** -0.5 * _LOG2E)).astype(_bf16)
    k = _dot(hb, w1_ref[:, _O_KD:_O_KD + GROUP_W])
    for hd in range(HEADS):
        sl = slice(hd * LANES, (hd + 1) * LANES)
        kd_ref[0, :, sl] = _rope(k[:, sl], cos, sin).astype(_bf16)
    vdt_ref[0] = _dot(hb, w1_ref[:, _O_VD:_O_VD + GROUP_W]).T.astype(_bf16)

    mscale = MLA_QK ** -0.5 * _LOG2E
    qn_ref[0] = (_dot(ql, wqn_ref[...]) * mscale).astype(_bf16)
    qr = _dot(ql, wqr_ref[...])
    for hd in range(HEADS):
        sl = slice(hd * LANES, (hd + 1) * LANES)
        qr_ref[0, :, sl] = (_rope(qr[:, sl], cos, sin) * mscale).astype(_bf16)
    kv = _dot(kvl, wkv_ref[...])
    kn_ref[0] = kv[:, :GROUP_W].astype(_bf16)
    vmt_ref[0] = kv[:, GROUP_W:].T.astype(_bf16)


def _proj(x, sc, sh, g, w1, gq, gkv, wqn, wqr, wkv, cos, sin):
    b, s, d = x.shape
    tm = min(512, s)
    row = lambda width: pl.BlockSpec((1, tm, width), lambda bi, i: (bi, i, 0))
    mod = pl.BlockSpec((1, 1, d), lambda bi, i: (bi, 0, 0))
    tab = pl.BlockSpec((tm, LANES), lambda bi, i: (i, 0))
    col = pl.BlockSpec((1, GROUP_W, tm), lambda bi, i: (bi, 0, i))
    rows = jax.ShapeDtypeStruct((b, s, GROUP_W), _bf16)
    cols = jax.ShapeDtypeStruct((b, GROUP_W, s), _bf16)
    return pl.pallas_call(
        _proj_kernel,
        grid=(b, s // tm),
        in_specs=[row(d), mod, mod, _const_spec(g.shape), _const_spec(w1.shape), _const_spec(gq.shape),
                  _const_spec(gkv.shape), _const_spec(wqn.shape), _const_spec(wqr.shape),
                  _const_spec(wkv.shape), tab, tab],
        out_specs=[row(GROUP_W), row(GROUP_W), col, row(GROUP_W), row(GROUP_W), row(GROUP_W), col, row(LANES)],
        out_shape=[rows, rows, cols, rows, rows, rows, cols, jax.ShapeDtypeStruct((b, s, LANES), _bf16)],
        compiler_params=_params(("arbitrary", "arbitrary")),
        name="proj",
    )(x, sc, sh, g, w1, gq, gkv, wqn, wqr, wkv, cos, sin)


def _attend_t(k, q, vt, finish):
    s = k.shape[0]
    ck = min(ATTN_KEY_CHUNK, s)
    n = s // ck
    scores = lambda c: _dot_nt(k[c * ck:(c + 1) * ck], q)

    def fold(o, m_o, pend):
        c, p, m_c = pend
        oc = _dot(vt[:, c * ck:(c + 1) * ck], p)
        return (oc if o is None else o * jnp.exp2(m_o - m_c) + oc), m_c

    m = l = o = m_o = pend = None
    st_next = scores(0)
    yield
    for c in range(n):
        st = st_next
        if c + 1 < n:
            st_next = scores(c + 1)
        if pend is not None:
            o, m_o = fold(o, m_o, pend)
        mc = jnp.max(st, axis=0, keepdims=True)
        m_new = mc if c == 0 else jnp.maximum(m, mc)
        p = jnp.exp2(st - m_new)
        lc = jnp.sum(p, axis=0, keepdims=True)
        l = lc if c == 0 else l * jnp.exp2(m - m_new) + lc
        m = m_new
        pend = (c, p.astype(_bf16), m_new)
        yield
    o, m_o = fold(o, m_o, pend)
    finish(o * (1.0 / l))


def _run_staggered(streams, newest_first=False):
    live = []
    pending = list(streams)
    while live or pending:
        if pending:
            live.append(pending.pop(0))
        for g in (list(reversed(live)) if newest_first else list(live)):
            try:
                next(g)
            except StopIteration:
                live.remove(g)


def _dattn_kernel(lq1_ref, lk1_ref, lq2_ref, lk2_ref, g_ref, q_ref, k_ref, vt_ref, o_ref, *, lam_init):
    tq = q_ref.shape[1]
    lam = (jnp.exp(jnp.sum(lq1_ref[...] * lk1_ref[...], axis=-1, keepdims=True))
           - jnp.exp(jnp.sum(lq2_ref[...] * lk2_ref[...], axis=-1, keepdims=True)) + lam_init)
    lane = lax.broadcasted_iota(jnp.int32, (tq, LANES), 1)
    first = lane < DIFF_QK

    def head(hd):
        sl = slice(hd * LANES, (hd + 1) * LANES)
        q = q_ref[0, :, sl]
        zero = jnp.zeros_like(q)

        def finish(o12):
            o = (o12[:, :tq] - lam * o12[:, tq:]).T
            o_ref[0, :, sl] = (_rms(o, g_ref[...]) * (1.0 - lam_init)).astype(_bf16)

        q12 = jnp.concatenate([jnp.where(first, q, zero), jnp.where(first, zero, q)], axis=0)
        return _attend_t(k_ref[0, :, sl], q12, vt_ref[0, sl, :], finish)

    _run_staggered([head(hd) for hd in range(ATTN_HEADS_PER_STEP)], newest_first=True)


def _dattn(qd, kd, vdt, lq1, lk1, lq2, lk2, g, lam_init):
    b, s, _ = qd.shape
    tq = min(ATTN_TQ, s)
    w = ATTN_HEADS_PER_STEP * LANES
    small = lambda a: pl.BlockSpec(a.shape, lambda bi, h, i: (0, 0))
    return pl.pallas_call(
        functools.partial(_dattn_kernel, lam_init=lam_init),
        grid=(b, HEADS // ATTN_HEADS_PER_STEP, s // tq),
        in_specs=[small(lq1), small(lk1), small(lq2), small(lk2), small(g),
                  pl.BlockSpec((1, tq, w), lambda bi, h, i: (bi, i, h)),
                  pl.BlockSpec((1, s, w), lambda bi, h, i: (bi, 0, h)),
                  pl.BlockSpec((1, w, s), lambda bi, h, i: (bi, h, 0))],
        out_specs=pl.BlockSpec((1, tq, w), lambda bi, h, i: (bi, i, h)),
        out_shape=jax.ShapeDtypeStruct((b, s, GROUP_W), _bf16),
        compiler_params=_params(("arbitrary",) * 3),
        name="dattn",
    )(lq1, lk1, lq2, lk2, g, qd, kd, vdt)


def _mattn_kernel(qn_ref, qr_ref, kn_ref, vt_ref, kpe_ref, o_ref):
    kpe = kpe_ref[0]

    def head(hd):
        sl = slice(hd * LANES, (hd + 1) * LANES)

        def finish(o):
            o_ref[0, :, sl] = o.T.astype(_bf16)

        q = jnp.concatenate([qn_ref[0, :, sl], qr_ref[0, :, sl]], axis=-1)
        k = jnp.concatenate([kn_ref[0, :, sl], kpe], axis=-1)
        return _attend_t(k, q, vt_ref[0, sl, :], finish)

    _run_staggered([head(hd) for hd in range(ATTN_HEADS_PER_STEP)], newest_first=True)


def _mattn(qn, qr, kn, vmt, kpe):
    b, s, _ = qn.shape
    tq = min(2 * ATTN_TQ, s)
    w = ATTN_HEADS_PER_STEP * LANES
    return pl.pallas_call(
        _mattn_kernel,
        grid=(b, HEADS // ATTN_HEADS_PER_STEP, s // tq),
        in_specs=[pl.BlockSpec((1, tq, w), lambda bi, h, i: (bi, i, h)),
                  pl.BlockSpec((1, tq, w), lambda bi, h, i: (bi, i, h)),
                  pl.BlockSpec((1, s, w), lambda bi, h, i: (bi, 0, h)),
                  pl.BlockSpec((1, w, s), lambda bi, h, i: (bi, h, 0)),
                  pl.BlockSpec((1, s, LANES), lambda bi, h, i: (bi, 0, 0))],
        out_specs=pl.BlockSpec((1, tq, w), lambda bi, h, i: (bi, i, h)),
        out_shape=jax.ShapeDtypeStruct((b, s, GROUP_W), _bf16),
        compiler_params=_params(("arbitrary",) * 3),
        name="mattn",
    )(qn, qr, kn, vmt, kpe)


def _post_kernel(x_ref, oa_ref, ob_ref, gt_ref, sc_ref, sh_ref, gmla_ref, gpost_ref, gpre_ref,
                 wo_ref, wr_ref, x1_ref, h2_ref, aff_ref):
    tm = x_ref.shape[1]

    def rows(r0):
        sl = slice(r0, r0 + POST_SUB)
        obn = _rms(ob_ref[0, sl, :].astype(_f32), gmla_ref[...]).astype(_bf16)
        m = _dot(oa_ref[0, sl, :], wo_ref[:GROUP_W, :]) + _dot(obn, wo_ref[GROUP_W:, :])
        yield
        x1 = x_ref[0, sl, :] + gt_ref[0] * _rms(m, gpost_ref[...])
        x1_ref[0, sl, :] = x1
        h2 = _rms(x1, gpre_ref[...]) * (1.0 + sc_ref[0]) + sh_ref[0]
        h2_ref[0, sl, :] = h2
        hh, hl = _split(h2)
        parts = _dot(hh, wr_ref[...]) + _dot(hl, wr_ref[...])
        yield
        logits = parts + pltpu.roll(parts, LANES - N_EXPERTS, 1)
        lane = lax.broadcasted_iota(jnp.int32, logits.shape, 1)
        logits = jnp.where(lane < N_EXPERTS, logits, -jnp.inf)
        p = jnp.exp(logits - jnp.max(logits, axis=-1, keepdims=True))
        aff = p / jnp.sum(p, axis=-1, keepdims=True)
        aff_ref[:, sl] = aff.T[0:N_EXPERTS, :]

    _run_staggered([rows(r0) for r0 in range(0, tm, POST_SUB)], newest_first=True)


def _post(x, oa, ob, gt, sc, sh, gmla, gpost, gpre, wo, wr):
    b, s, d = x.shape
    tm = min(POST_STREAMS * POST_SUB, s)
    nt = s // tm
    row = lambda width: pl.BlockSpec((1, tm, width), lambda bi, i: (bi, i, 0))
    mod = pl.BlockSpec((1, 1, d), lambda bi, i: (bi, 0, 0))
    return pl.pallas_call(
        _post_kernel,
        grid=(b, nt),
        in_specs=[row(d), row(GROUP_W), row(GROUP_W), mod, mod, mod, _const_spec(gmla.shape),
                  _const_spec(gpost.shape), _const_spec(gpre.shape), _const_spec(wo.shape),
                  _const_spec(wr.shape)],
        out_specs=[row(d), row(d), pl.BlockSpec((N_EXPERTS, tm), lambda bi, i: (0, bi * nt + i))],
        out_shape=[jax.ShapeDtypeStruct((b, s, d), _f32), jax.ShapeDtypeStruct((b, s, d), _f32),
                   jax.ShapeDtypeStruct((N_EXPERTS, b * s), _f32)],
        compiler_params=_params(("arbitrary", "arbitrary")),
        name="post",
    )(x, oa, ob, gt, sc, sh, gmla, gpost, gpre, wo, wr)


def _route_kernel(aff_ref, idx_ref, gate_ref, selpos_ref, bex_ref, gt_scr, eq_scr, need_scr, *, cap, n_tok):
    nb = ROUTE_BLOCKS
    aff = aff_ref[...]

    def count_ge(t):
        c = jnp.sum(jnp.where(aff >= t, 1.0, 0.0), axis=2, keepdims=True)
        return jnp.sum(c, axis=1, keepdims=True)

    hi = jnp.full((N_EXPERTS, 1, 1), 2.0, _f32)
    for shift in (64, 32, 16, 8, 4, 2, 1):
        cand = hi * (2.0 ** -shift)
        hi = jnp.where(count_ge(cand) < cap, cand, hi)
    lo = jnp.where(count_ge(hi * 0.5) >= cap, hi * 0.5, 0.0)

    def bisect(i, bracket):
        lo, hi = bracket
        mid = 0.5 * (lo + hi)
        keep = count_ge(mid) >= cap
        return jnp.where(keep, mid, lo), jnp.where(keep, hi, mid)

    lo, hi = lax.fori_loop(0, ROUTE_BISECTIONS, bisect, (lo, hi))
    above = jnp.where(aff >= hi, 1.0, 0.0)
    gt_scr[...] = above
    eq_scr[...] = jnp.where(aff >= lo, 1.0, 0.0) - above
    need_scr[...] = jnp.broadcast_to(cap - count_ge(hi), need_scr.shape)

    r_io = lax.broadcasted_iota(jnp.int32, (nb, LANES), 0)
    c_io = lax.broadcasted_iota(jnp.int32, (nb, LANES), 1)
    triu = jnp.where(r_io <= c_io, 1.0, 0.0).astype(_bf16)
    tril = jnp.where(c_io <= r_io, 1.0, 0.0).astype(_bf16)
    reps = cap // LANES
    slot = lax.broadcasted_iota(jnp.int32, (nb, cap), 1).astype(_f32)
    row = lax.broadcasted_iota(jnp.int32, (nb, cap), 0).astype(_f32)

    def prefix(m):
        loc = _dot(m.astype(_bf16), triu)
        tot = jnp.broadcast_to(loc[:, LANES - 1:LANES], loc.shape)
        return loc, tot, _dot(tril, tot.astype(_bf16))

    def per_expert(e, carry):
        gte = gt_scr[e]
        eqe = eq_scr[e]
        need = need_scr[e][0:1, :]
        eloc, etot, ebinc = prefix(eqe)
        eq_before = ebinc - etot + eloc - eqe
        sel = jnp.maximum(gte, eqe * jnp.where(eq_before < need, 1.0, 0.0))
        loc, tot, binc = prefix(sel)
        bexc = binc - tot
        pos = bexc + loc - sel
        selpos_ref[e] = jnp.where(sel > 0.0, pos, -1.0).astype(jnp.int32)
        bex_ref[e] = bexc.T[0:8, :]

        binc_t = jnp.tile(binc, (1, reps))
        bexc_t = jnp.tile(bexc, (1, reps))
        blk = jnp.sum(jnp.where(binc_t <= slot, 1.0, 0.0), axis=0, keepdims=True)
        hit = row == blk
        onehot = jnp.where(hit, 1.0, 0.0).astype(_bf16)
        rank = slot[0:1, :] - jnp.sum(jnp.where(hit, bexc_t, 0.0), axis=0, keepdims=True)
        loc_rows = _dot(loc.T.astype(_bf16), onehot)
        within = jnp.sum(jnp.where(loc_rows <= rank, 1.0, 0.0), axis=0, keepdims=True)
        token = jnp.minimum(blk * LANES + within, n_tok - 1.0)
        idx_ref[e] = jnp.broadcast_to(token, (8, cap)).astype(jnp.int32)

        at = aff_ref[e].T
        a0 = at.astype(_bf16)
        r1 = at - a0.astype(_f32)
        a1 = r1.astype(_bf16)
        a2 = (r1 - a1.astype(_f32)).astype(_bf16)
        aff_rows = _dot(a0, onehot) + _dot(a1, onehot) + _dot(a2, onehot)
        gate = jnp.sum(jnp.where(row == within, aff_rows, 0.0), axis=0, keepdims=True)
        gate_ref[e] = jnp.broadcast_to(gate, (8, cap))
        return carry

    lax.fori_loop(0, N_EXPERTS, per_expert, 0)


def _route(aff3, cap, n_tok):
    nb = ROUTE_BLOCKS
    full = lambda shape: pl.BlockSpec(shape, lambda: (0,) * len(shape))
    return pl.pallas_call(
        functools.partial(_route_kernel, cap=cap, n_tok=n_tok),
        in_specs=[full((N_EXPERTS, nb, LANES))],
        out_specs=[full((N_EXPERTS, 8, cap)), full((N_EXPERTS, 8, cap)), full((N_EXPERTS, nb, LANES)),
                   full((N_EXPERTS, 8, LANES))],
        out_shape=[jax.ShapeDtypeStruct((N_EXPERTS, 8, cap), jnp.int32),
                   jax.ShapeDtypeStruct((N_EXPERTS, 8, cap), _f32),
                   jax.ShapeDtypeStruct((N_EXPERTS, nb, LANES), jnp.int32),
                   jax.ShapeDtypeStruct((N_EXPERTS, 8, LANES), _f32)],
        scratch_shapes=[pltpu.VMEM((N_EXPERTS, nb, LANES), _f32), pltpu.VMEM((N_EXPERTS, nb, LANES), _f32),
                        pltpu.VMEM((N_EXPERTS, 8, LANES), _f32)],
        compiler_params=pltpu.CompilerParams(vmem_limit_bytes=VMEM_LIMIT),
        name="route",
    )(aff3)


def _ffn_kernel(idx0_ref, idxn_ref, gate_ref, h_hbm, wg_ref, wu_ref, wd_ref, o_ref, rows_scr, xe_scr, acc_scr,
                sem, *, tc, n_tiles, nf):
    f = pl.program_id(2)
    g = pl.program_id(0) * pl.num_programs(1) + pl.program_id(1)
    per_step = tc // nf

    def row_copy(idx_ref, r):
        t = idx_ref[0, 0, r]
        return pltpu.make_async_copy(h_hbm.at[pl.ds(t, 1), :], rows_scr.at[pl.ds(r, 1), :], sem)

    @pl.when((g == 0) & (f == 0))
    def _():
        def issue(r, carry):
            row_copy(idx0_ref, r).start()
            return carry

        lax.fori_loop(0, tc, issue, 0)

    @pl.when(f == 0)
    def _():
        pltpu.make_async_copy(h_hbm.at[pl.ds(0, tc), :], rows_scr, sem).wait()
        xe_scr[...] = rows_scr[...].astype(_bf16)
        acc_scr[...] = jnp.zeros_like(acc_scr)

    xe = xe_scr[...]
    hg = _dot(xe, wg_ref[...].astype(_bf16))
    hu = _dot(xe, wu_ref[...].astype(_bf16))
    a = (hg * (1.0 / (1.0 + jnp.exp(-hg))) * hu).astype(_bf16)
    acc_scr[...] += _dot(a, wd_ref[...].astype(_bf16))

    @pl.when(g + 1 < n_tiles)
    def _():
        for r in range(per_step):
            row_copy(idxn_ref, f * per_step + r).start()

    @pl.when(f == nf - 1)
    def _():
        gate_col = jnp.broadcast_to(gate_ref[0, 0:1, :], (LANES, tc)).T[:, 0:1]
        o_ref[0] = (acc_scr[...] * gate_col).astype(_bf16)


def _ffn(idx3, gate3, h2, w_gate, w_up, w_down, cap):
    n, d = h2.shape
    tc = min(1024, cap)
    nc = cap // tc
    tf = min(256, D_FF)
    n_tiles = N_EXPERTS * nc
    idx_spec = lambda off: pl.BlockSpec(
        (1, 1, tc), lambda e, i, f: (jnp.minimum(e * nc + i + off, n_tiles - 1), 0, 0), memory_space=pltpu.SMEM)
    return pl.pallas_call(
        functools.partial(_ffn_kernel, tc=tc, n_tiles=n_tiles, nf=D_FF // tf),
        grid=(N_EXPERTS, nc, D_FF // tf),
        in_specs=[idx_spec(0), idx_spec(1),
                  pl.BlockSpec((1, 8, tc), lambda e, i, f: (e, 0, i)),
                  pl.BlockSpec(memory_space=pl.ANY),
                  pl.BlockSpec((None, d, tf), lambda e, i, f: (e, 0, f)),
                  pl.BlockSpec((None, d, tf), lambda e, i, f: (e, 0, f)),
                  pl.BlockSpec((None, tf, d), lambda e, i, f: (e, f, 0))],
        out_specs=pl.BlockSpec((1, tc, d), lambda e, i, f: (e, i, 0)),
        out_shape=jax.ShapeDtypeStruct((N_EXPERTS, cap, d), _bf16),
        scratch_shapes=[pltpu.VMEM((tc, d), _f32), pltpu.VMEM((tc, d), _bf16),
                        pltpu.VMEM((tc, d), _f32), pltpu.SemaphoreType.DMA(())],
        compiler_params=_params(("arbitrary",) * 3),
        name="ffn",
    )(idx3, idx3, gate3, h2, w_gate, w_up, w_down)


_WIN = LANES
_ALIGN = 16
_CHUNK = _WIN - _ALIGN


def _combine_kernel(base_ref, basen_ref, cnt_ref, pos_ref, x1_ref, gt_ref, g_ref, ye_hbm, o_ref, stage, xstage,
                    y_scr, sems, xsem, *, cap):
    tt = pos_ref.shape[0]
    d = stage.shape[3]
    t = pl.program_id(0) * pl.num_programs(1) + pl.program_id(1)
    n_t = pl.num_programs(0) * pl.num_programs(1)
    slot = t & 1
    r_io = lax.broadcasted_iota(jnp.int32, (tt, _WIN), 1)

    def window(lo):
        aligned = lax.shift_left(lax.shift_right_logical(lo, 4), 4)
        return pl.multiple_of(jnp.minimum(aligned, cap - _WIN), _ALIGN)

    def copy(e, ws, dst_slot):
        return pltpu.make_async_copy(ye_hbm.at[e, pl.ds(ws, _WIN), :], stage.at[dst_slot, e], sems.at[dst_slot])

    def onehot(e, ws, lo):
        want = jnp.where((r_io[0:1] >= lo - ws) & (r_io[0:1] < lo - ws + _CHUNK), r_io[0:1] + ws, -2)
        p = jnp.broadcast_to(pos_ref[:, e:e + 1], (tt, _WIN))
        return jnp.where(p == want, 1.0, 0.0).astype(_bf16)

    @pl.when(t == 0)
    def _():
        for e in range(N_EXPERTS):
            copy(e, window(base_ref[0, 0, e]), 0).start()

    @pl.when(t + 1 < n_t)
    def _():
        for e in range(N_EXPERTS):
            copy(e, window(basen_ref[0, 0, e]), 1 - slot).start()

    los = [base_ref[0, 0, e] for e in range(N_EXPERTS)]
    oh = jnp.concatenate([onehot(e, window(lo), lo) for e, lo in enumerate(los)], axis=1)
    for e, lo in enumerate(los):
        copy(e, window(lo), slot).wait()
    y_scr[...] = _dot(oh, stage[slot].reshape(N_EXPERTS * _WIN, d))

    for e in range(N_EXPERTS):
        cnt = cnt_ref[0, 0, e]
        n_chunks = sum((cnt > kc * _CHUNK).astype(jnp.int32) for kc in range(-(-tt // _CHUNK)))

        def extra(kc, carry, e=e):
            lo = los[e] + kc * _CHUNK
            ws = window(lo)
            cp = pltpu.make_async_copy(ye_hbm.at[e, pl.ds(ws, _WIN), :], xstage, xsem)
            cp.start()
            cp.wait()
            y_scr[...] += _dot(onehot(e, ws, lo), xstage[...])
            return carry

        lax.fori_loop(1, n_chunks, extra, 0)

    o_ref[0] = x1_ref[0] + gt_ref[0] * _rms(y_scr[...], g_ref[...])


def _combine(base, cnt, pos, x1, gt, g, ye, cap):
    b, s, d = x1.shape
    tt = min(512, s)
    nt = s // tt
    n_t = b * nt
    mod = pl.BlockSpec((1, 1, d), lambda bi, i: (bi, 0, 0))
    smem = lambda off: pl.BlockSpec(
        (1, 1, N_EXPERTS), lambda bi, i: (jnp.minimum(bi * nt + i + off, n_t - 1), 0, 0), memory_space=pltpu.SMEM)
    return pl.pallas_call(
        functools.partial(_combine_kernel, cap=cap),
        grid=(b, nt),
        in_specs=[smem(0), smem(1), smem(0),
                  pl.BlockSpec((tt, N_EXPERTS), lambda bi, i: (bi * nt + i, 0)),
                  pl.BlockSpec((1, tt, d), lambda bi, i: (bi, i, 0)), mod, _const_spec(g.shape),
                  pl.BlockSpec(memory_space=pl.ANY)],
        out_specs=pl.BlockSpec((1, tt, d), lambda bi, i: (bi, i, 0)),
        out_shape=jax.ShapeDtypeStruct((b, s, d), _f32),
        scratch_shapes=[pltpu.VMEM((2, N_EXPERTS, _WIN, d), _bf16), pltpu.VMEM((_WIN, d), _bf16),
                        pltpu.VMEM((tt, d), _f32), pltpu.SemaphoreType.DMA((2,)), pltpu.SemaphoreType.DMA(())],
        compiler_params=_params(("arbitrary", "arbitrary")),
        name="combine",
    )(base, base, cnt, pos, x1, gt, g, ye)


def _prep_weights(w_in, w_uq, w_ukv, w_out, w_router):
    w1 = w_in.astype(_bf16)
    wq = w_uq.astype(_bf16).reshape(Q_LORA, HEADS, MLA_QK)
    wqn = wq[:, :, :HEAD_W].reshape(Q_LORA, GROUP_W)
    wqr = jnp.pad(wq[:, :, HEAD_W:], ((0, 0), (0, 0), (0, LANES - MLA_ROPE))).reshape(Q_LORA, HEADS * LANES)
    wkv = w_ukv.astype(_bf16).reshape(KV_LORA, HEADS, 2, HEAD_W).transpose(0, 2, 1, 3)
    wkv = wkv.reshape(KV_LORA, 2 * GROUP_W)
    wo = w_out.astype(_bf16)
    wrh, wrl = _split(w_router)
    wr2 = jnp.pad(jnp.concatenate([wrh, wrl], axis=1), ((0, 0), (0, LANES - 2 * N_EXPERTS)))
    return w1, wqn, wqr, wkv, wo, wr2


def _rope_tables(s):
    half = DIFF_QK // 2
    inv = ROPE_THETA ** (-jnp.arange(half, dtype=_f32) / half)
    ang = jnp.arange(s, dtype=_f32)[:, None] * inv[None, :]
    cos = jnp.cos(ang)
    sin = jnp.sin(ang)
    return jnp.tile(cos, (1, 4)), jnp.concatenate([-sin, sin, -sin, sin], axis=1)


def _layer(x, mod, wts, lam_init):
    (g_pre_attn, g_post_attn, g_pre_ffn, g_post_ffn, lq1, lk1, lq2, lk2, g_diff_sub, g_q_lat, g_kv_lat,
     g_mla_out, w1, wqn, wqr, wkv, wo, wr, w_gate, w_up, w_down) = wts
    b, s, d = x.shape
    n = b * s
    cap = CAPACITY_FACTOR * n // N_EXPERTS
    sh_a, sc_a, gt_a, sh_f, sc_f, gt_f = [mod[:, None, i * d:(i + 1) * d] for i in range(6)]
    cos, sin = _rope_tables(s)

    qd, kd, vdt, qn, qr, kn, vmt, kpe = _proj(x, sc_a, sh_a, g_pre_attn, w1, g_q_lat, g_kv_lat, wqn, wqr, wkv,
                                              cos, sin)
    oa = _dattn(qd, kd, vdt, lq1, lk1, lq2, lk2, g_diff_sub, lam_init)
    ob = _mattn(qn, qr, kn, vmt, kpe)
    x1, h2, aff_t = _post(x, oa, ob, gt_a, sc_f, sh_f, g_mla_out, g_post_attn, g_pre_ffn, wo, wr)

    n_pad = ROUTE_BLOCKS * LANES
    aff3 = jnp.pad(aff_t, ((0, 0), (0, n_pad - n))).reshape(N_EXPERTS, ROUTE_BLOCKS, LANES)
    idx8, gate8, selpos, bex8 = _route(aff3, cap, n)

    tc = min(1024, cap)
    idx3 = idx8[:, 0, :].reshape(N_EXPERTS * (cap // tc), 1, tc)
    ye = _ffn(idx3, gate8, h2.reshape(n, d), w_gate, w_up, w_down, cap)

    tt = min(512, s)
    n_tiles = n // tt
    bex = bex8[:, 0, :].astype(jnp.int32).T
    base = bex[::tt // LANES][:n_tiles]
    cnt = jnp.concatenate([base[1:], jnp.full((1, N_EXPERTS), cap, jnp.int32)], axis=0) - base
    pos = selpos.reshape(N_EXPERTS, n_pad).T[:n]
    return _combine(base.reshape(n_tiles, 1, N_EXPERTS), cnt.reshape(n_tiles, 1, N_EXPERTS), pos, x1, gt_f,
                    g_post_ffn, ye, cap)


def kernel(x_prompt, x_sample, c_prompt, c_sample, w_ada, b_ada, g_pre_attn, g_post_attn, g_pre_ffn, g_post_ffn, w_in, lam_q1, lam_k1, lam_q2, lam_k2, g_diff_sub, g_q_lat, w_uq, g_kv_lat, w_ukv, g_mla_out, w_out, w_router, w_gate, w_up, w_down):
    y_prompt, y_sample = x_prompt, x_sample
    bp = c_prompt.shape[0]
    bs = c_sample.shape[0]
    rows = -(-(bp + bs) // 8) * 8
    c_all = jnp.concatenate([c_prompt, c_sample, jnp.zeros((rows - bp - bs, c_prompt.shape[1]), _f32)], axis=0)
    for l in range(DEPTH):
        lam_init = 0.8 - 0.6 * math.exp(-0.3 * l)
        mod = _ada(c_all, w_ada[l], b_ada[l][None, :])
        row = lambda a: a[l][None, :]
        wts = (row(g_pre_attn), row(g_post_attn), row(g_pre_ffn), row(g_post_ffn), row(lam_q1), row(lam_k1),
               row(lam_q2), row(lam_k2), row(g_diff_sub), row(g_q_lat), row(g_kv_lat), row(g_mla_out),
               *_prep_weights(w_in[l], w_uq[l], w_ukv[l], w_out[l], w_router[l]),
               w_gate[l], w_up[l], w_down[l])
        y_prompt = _layer(y_prompt, mod[:bp], wts, lam_init)
        y_sample = _layer(y_sample, mod[bp:bp + bs], wts, lam_init)
    return (y_prompt, y_sample)
```

```python
import functools
import math

import jax
import jax.numpy as jnp
from jax import lax
from jax.experimental import pallas as pl
from jax.experimental.pallas import tpu as pltpu

D_MODEL = 2048
DEPTH = 1
ROPE_THETA = 10000.0
NORM_EPS = 1e-6

HEADS = D_MODEL // 256
DIFF_QK = 64
HEAD_W = 128
MLA_ROPE = 64
MLA_QK = HEAD_W + MLA_ROPE
Q_LORA = D_MODEL // 4
KV_LORA = D_MODEL // 8
GROUP_W = HEADS * HEAD_W
N_EXPERTS = 16
CAPACITY_FACTOR = 2
D_FF = D_MODEL

LANES = 128
ROUTE_BLOCKS = 128
ROUTE_BISECTIONS = 40
VMEM_LIMIT = 56 * 1024 * 1024
ATTN_TQ = 512
ATTN_KEY_CHUNK = 1024
POST_SUB = 256
POST_STREAMS = 2
ATTN_HEADS_PER_STEP = 4

_O_QD = 0
_O_KD = _O_QD + GROUP_W
_O_VD = _O_KD + GROUP_W
_O_MQ = _O_VD + GROUP_W
_O_MKV = _O_MQ + Q_LORA
_O_KR = _O_MKV + KV_LORA
_W1_COLS = _O_KR + LANES

_f32 = jnp.float32
_bf16 = jnp.bfloat16
_LOG2E = math.log2(math.e)


def _dot(a, b):
    return jnp.dot(a, b, preferred_element_type=_f32)


def _dot_nt(a, b):
    return lax.dot_general(a, b, (((1,), (1,)), ((), ())), preferred_element_type=_f32)


def _split(x):
    hi = x.astype(_bf16)
    lo = (x - hi.astype(_f32)).astype(_bf16)
    return hi, lo


def _dot3(a, b):
    ah, al = _split(a)
    bh, bl = _split(b)
    return _dot(ah, bh) + _dot(al, bh) + _dot(ah, bl)


def _rms(x, g):
    return x * lax.rsqrt(jnp.mean(x * x, axis=-1, keepdims=True) + NORM_EPS) * g


def _rope(x, cos, sin):
    lane = lax.broadcasted_iota(jnp.int32, x.shape, 1)
    partner = jnp.where((lane & 63) < 32, pltpu.roll(x, LANES - 32, 1), pltpu.roll(x, 32, 1))
    return x * cos + partner * sin


def _params(sem, vmem=VMEM_LIMIT):
    return pltpu.CompilerParams(dimension_semantics=sem, vmem_limit_bytes=vmem)


def _const_spec(shape):
    nd = len(shape)
    return pl.BlockSpec(shape, lambda *_: (0,) * nd, pipeline_mode=pl.Buffered(1))


def _ada_kernel(c_ref, w_ref, b_ref, o_ref):
    c = c_ref[...]
    s = c * (1.0 / (1.0 + jnp.exp(-c)))
    o_ref[...] = _dot3(s, w_ref[...]) + b_ref[...]


def _ada(c, w, b):
    rows, d = c.shape
    cols = w.shape[1]
    tn = 1024
    return pl.pallas_call(
        _ada_kernel,
        grid=(cols // tn,),
        in_specs=[pl.BlockSpec((rows, d), lambda j: (0, 0)),
                  pl.BlockSpec((d, tn), lambda j: (0, j)),
                  pl.BlockSpec((1, tn), lambda j: (0, j))],
        out_specs=pl.BlockSpec((rows, tn), lambda j: (0, j)),
        out_shape=jax.ShapeDtypeStruct((rows, cols), _f32),
        compiler_params=_params(("arbitrary",)),
        name="ada",
    )(c, w, b)


def _proj_kernel(x_ref, sc_ref, sh_ref, g_ref, w1_ref, gq_ref, gkv_ref, wqn_ref, wqr_ref, wkv_ref,
                 cos_ref, sin_ref, qd_ref, kd_ref, vdt_ref, qn_ref, qr_ref, kn_ref, vmt_ref, kpe_ref):
    x = x_ref[0]
    h = _rms(x, g_ref[...]) * (1.0 + sc_ref[0]) + sh_ref[0]
    hb = h.astype(_bf16)
    cos = cos_ref[...]
    sin = sin_ref[...]

    ql = _rms(_dot(hb, w1_ref[:, _O_MQ:_O_MQ + Q_LORA]), gq_ref[...]).astype(_bf16)
    kvl = _rms(_dot(hb, w1_ref[:, _O_MKV:_O_MKV + KV_LORA]), gkv_ref[...]).astype(_bf16)
    kr = _dot(hb, w1_ref[:, _O_KR:_O_KR + MLA_ROPE])
    kr = jnp.concatenate([kr, jnp.zeros_like(kr)], axis=1)
    kpe_ref[0] = _rope(kr, cos, sin).astype(_bf16)

    q = _dot(hb, w1_ref[:, _O_QD:_O_QD + GROUP_W])
    for hd in range(HEADS):
        sl = slice(hd * LANES, (hd + 1) * LANES)
        qd_ref[0, :, sl] = (_rope(q[:, sl], cos, sin) * (DIFF_QK ** -0.5 * _LOG2E)).astype(_bf16)
    k = _dot(hb, w1_ref[:, _O_KD:_O_KD + GROUP_W])
    for hd in range(HEADS):
        sl = slice(hd * LANES, (hd + 1) * LANES)
        kd_ref[0, :, sl] = _rope(k[:, sl], cos, sin).astype(_bf16)
    vdt_ref[0] = _dot(hb, w1_ref[:, _O_VD:_O_VD + GROUP_W]).T.astype(_bf16)

    mscale = MLA_QK ** -0.5 * _LOG2E
    qn_ref[0] = (_dot(ql, wqn_ref[...]) * mscale).astype(_bf16)
    qr = _dot(ql, wqr_ref[...])
    for hd in range(HEADS):
        sl = slice(hd * LANES, (hd + 1) * LANES)
        qr_ref[0, :, sl] = (_rope(qr[:, sl], cos, sin) * mscale).astype(_bf16)
    kv = _dot(kvl, wkv_ref[...])
    kn_ref[0] = kv[:, :GROUP_W].astype(_bf16)
    vmt_ref[0] = kv[:, GROUP_W:].T.astype(_bf16)


def _proj(x, sc, sh, g, w1, gq, gkv, wqn, wqr, wkv, cos, sin):
    b, s, d = x.shape
    tm = min(512, s)
    row = lambda width: pl.BlockSpec((1, tm, width), lambda bi, i: (bi, i, 0))
    mod = pl.BlockSpec((1, 1, d), lambda bi, i: (bi, 0, 0))
    tab = pl.BlockSpec((tm, LANES), lambda bi, i: (i, 0))
    col = pl.BlockSpec((1, GROUP_W, tm), lambda bi, i: (bi, 0, i))
    rows = jax.ShapeDtypeStruct((b, s, GROUP_W), _bf16)
    cols = jax.ShapeDtypeStruct((b, GROUP_W, s), _bf16)
    return pl.pallas_call(
        _proj_kernel,
        grid=(b, s // tm),
        in_specs=[row(d), mod, mod, _const_spec(g.shape), _const_spec(w1.shape), _const_spec(gq.shape),
                  _const_spec(gkv.shape), _const_spec(wqn.shape), _const_spec(wqr.shape),
                  _const_spec(wkv.shape), tab, tab],
        out_specs=[row(GROUP_W), row(GROUP_W), col, row(GROUP_W), row(GROUP_W), row(GROUP_W), col, row(LANES)],
        out_shape=[rows, rows, cols, rows, rows, rows, cols, jax.ShapeDtypeStruct((b, s, LANES), _bf16)],
        compiler_params=_params(("arbitrary", "arbitrary")),
        name="proj",
    )(x, sc, sh, g, w1, gq, gkv, wqn, wqr, wkv, cos, sin)


def _attend_t(k, q, vt, finish):
    s = k.shape[0]
    ck = min(ATTN_KEY_CHUNK, s)
    n = s // ck
    scores = lambda c: _dot_nt(k[c * ck:(c + 1) * ck], q)

    def fold(o, m_o, pend):
        c, p, m_c = pend
        oc = _dot(vt[:, c * ck:(c + 1) * ck], p)
        return (oc if o is None else o * jnp.exp2(m_o - m_c) + oc), m_c

    m = l = o = m_o = pend = None
    st_next = scores(0)
    yield
    for c in range(n):
        st = st_next
        if c + 1 < n:
            st_next = scores(c + 1)
        if pend is not None:
            o, m_o = fold(o, m_o, pend)
        mc = jnp.max(st, axis=0, keepdims=True)
        m_new = mc if c == 0 else jnp.maximum(m, mc)
        p = jnp.exp2(st - m_new)
        lc = jnp.sum(p, axis=0, keepdims=True)
        l = lc if c == 0 else l * jnp.exp2(m - m_new) + lc
        m = m_new
        pend = (c, p.astype(_bf16), m_new)
        yield
    o, m_o = fold(o, m_o, pend)
    finish(o * (1.0 / l))


def _run_staggered(streams, newest_first=False):
    live = []
    pending = list(streams)
    while live or pending:
        if pending:
            live.append(pending.pop(0))
        for g in (list(reversed(live)) if newest_first else list(live)):
            try:
                next(g)
            except StopIteration:
                live.remove(g)


def _dattn_kernel(lq1_ref, lk1_ref, lq2_ref, lk2_ref, g_ref, q_ref, k_ref, vt_ref, o_ref, *, lam_init):
    tq = q_ref.shape[1]
    lam = (jnp.exp(jnp.sum(lq1_ref[...] * lk1_ref[...], axis=-1, keepdims=True))
           - jnp.exp(jnp.sum(lq2_ref[...] * lk2_ref[...], axis=-1, keepdims=True)) + lam_init)
    lane = lax.broadcasted_iota(jnp.int32, (tq, LANES), 1)
    first = lane < DIFF_QK

    def head(hd):
        sl = slice(hd * LANES, (hd + 1) * LANES)
        q = q_ref[0, :, sl]
        zero = jnp.zeros_like(q)

        def finish(o12):
            o = (o12[:, :tq] - lam * o12[:, tq:]).T
            o_ref[0, :, sl] = (_rms(o, g_ref[...]) * (1.0 - lam_init)).astype(_bf16)

        q12 = jnp.concatenate([jnp.where(first, q, zero), jnp.where(first, zero, q)], axis=0)
        return _attend_t(k_ref[0, :, sl], q12, vt_ref[0, sl, :], finish)

    _run_staggered([head(hd) for hd in range(ATTN_HEADS_PER_STEP)], newest_first=True)


def _dattn(qd, kd, vdt, lq1, lk1, lq2, lk2, g, lam_init):
    b, s, _ = qd.shape
    tq = min(ATTN_TQ, s)
    w = ATTN_HEADS_PER_STEP * LANES
    small = lambda a: pl.BlockSpec(a.shape, lambda bi, h, i: (0, 0))
    return pl.pallas_call(
        functools.partial(_dattn_kernel, lam_init=lam_init),
        grid=(b, HEADS // ATTN_HEADS_PER_STEP, s // tq),
        in_specs=[small(lq1), small(lk1), small(lq2), small(lk2), small(g),
                  pl.BlockSpec((1, tq, w), lambda bi, h, i: (bi, i, h)),
                  pl.BlockSpec((1, s, w), lambda bi, h, i: (bi, 0, h)),
                  pl.BlockSpec((1, w, s), lambda bi, h, i: (bi, h, 0))],
        out_specs=pl.BlockSpec((1, tq, w), lambda bi, h, i: (bi, i, h)),
        out_shape=jax.ShapeDtypeStruct((b, s, GROUP_W), _bf16),
        compiler_params=_params(("arbitrary",) * 3),
        name="dattn",
    )(lq1, lk1, lq2, lk2, g, qd, kd, vdt)


def _mattn_kernel(qn_ref, qr_ref, kn_ref, vt_ref, kpe_ref, o_ref):
    kpe = kpe_ref[0]

    def head(hd):
        sl = slice(hd * LANES, (hd + 1) * LANES)

        def finish(o):
            o_ref[0, :, sl] = o.T.astype(_bf16)

        q = jnp.concatenate([qn_ref[0, :, sl], qr_ref[0, :, sl]], axis=-1)
        k = jnp.concatenate([kn_ref[0, :, sl], kpe], axis=-1)
        return _attend_t(k, q, vt_ref[0, sl, :], finish)

    _run_staggered([head(hd) for hd in range(ATTN_HEADS_PER_STEP)], newest_first=True)


def _mattn(qn, qr, kn, vmt, kpe):
    b, s, _ = qn.shape
    tq = min(2 * ATTN_TQ, s)
    w = ATTN_HEADS_PER_STEP * LANES
    return pl.pallas_call(
        _mattn_kernel,
        grid=(b, HEADS // ATTN_HEADS_PER_STEP, s // tq),
        in_specs=[pl.BlockSpec((1, tq, w), lambda bi, h, i: (bi, i, h)),
                  pl.BlockSpec((1, tq, w), lambda bi, h, i: (bi, i, h)),
                  pl.BlockSpec((1, s, w), lambda bi, h, i: (bi, 0, h)),
                  pl.BlockSpec((1, w, s), lambda bi, h, i: (bi, h, 0)),
                  pl.BlockSpec((1, s, LANES), lambda bi, h, i: (bi, 0, 0))],
        out_specs=pl.BlockSpec((1, tq, w), lambda bi, h, i: (bi, i, h)),
        out_shape=jax.ShapeDtypeStruct((b, s, GROUP_W), _bf16),
        compiler_params=_params(("arbitrary",) * 3),
        name="mattn",
    )(qn, qr, kn, vmt, kpe)


def _post_kernel(x_ref, oa_ref, ob_ref, gt_ref, sc_ref, sh_ref, gmla_ref, gpost_ref, gpre_ref,
                 wo_ref, wr_ref, x1_ref, h2_ref, aff_ref):
    tm = x_ref.shape[1]

    def rows(r0):
        sl = slice(r0, r0 + POST_SUB)
        obn = _rms(ob_ref[0, sl, :].astype(_f32), gmla_ref[...]).astype(_bf16)
        m = _dot(oa_ref[0, sl, :], wo_ref[:GROUP_W, :]) + _dot(obn, wo_ref[GROUP_W:, :])
        yield
        x1 = x_ref[0, sl, :] + gt_ref[0] * _rms(m, gpost_ref[...])
        x1_ref[0, sl, :] = x1
        h2 = _rms(x1, gpre_ref[...]) * (1.0 + sc_ref[0]) + sh_ref[0]
        h2_ref[0, sl, :] = h2
        hh, hl = _split(h2)
        parts = _dot(hh, wr_ref[...]) + _dot(hl, wr_ref[...])
        yield
        logits = parts + pltpu.roll(parts, LANES - N_EXPERTS, 1)
        lane = lax.broadcasted_iota(jnp.int32, logits.shape, 1)
        logits = jnp.where(lane < N_EXPERTS, logits, -jnp.inf)
        p = jnp.exp(logits - jnp.max(logits, axis=-1, keepdims=True))
        aff = p / jnp.sum(p, axis=-1, keepdims=True)
        aff_ref[:, sl] = aff.T[0:N_EXPERTS, :]

    _run_staggered([rows(r0) for r0 in range(0, tm, POST_SUB)], newest_first=True)


def _post(x, oa, ob, gt, sc, sh, gmla, gpost, gpre, wo, wr):
    b, s, d = x.shape
    tm = min(POST_STREAMS * POST_SUB, s)
    nt = s // tm
    row = lambda width: pl.BlockSpec((1, tm, width), lambda bi, i: (bi, i, 0))
    mod = pl.BlockSpec((1, 1, d), lambda bi, i: (bi, 0, 0))
    return pl.pallas_call(
        _post_kernel,
        grid=(b, nt),
        in_specs=[row(d), row(GROUP_W), row(GROUP_W), mod, mod, mod, _const_spec(gmla.shape),
                  _const_spec(gpost.shape), _const_spec(gpre.shape), _const_spec(wo.shape),
                  _const_spec(wr.shape)],
        out_specs=[row(d), row(d), pl.BlockSpec((N_EXPERTS, tm), lambda bi, i: (0, bi * nt + i))],
        out_shape=[jax.ShapeDtypeStruct((b, s, d), _f32), jax.ShapeDtypeStruct((b, s, d), _f32),
                   jax.ShapeDtypeStruct((N_EXPERTS, b * s), _f32)],
        compiler_params=_params(("arbitrary", "arbitrary")),
        name="post",
    )(x, oa, ob, gt, sc, sh, gmla, gpost, gpre, wo, wr)


def _route_kernel(aff_ref, idx_ref, gate_ref, selpos_ref, bex_ref, gt_scr, eq_scr, need_scr, *, cap, n_tok):
    nb = ROUTE_BLOCKS
    aff = aff_ref[...]

    def count_ge(t):
        c = jnp.sum(jnp.where(aff >= t, 1.0, 0.0), axis=2, keepdims=True)
        return jnp.sum(c, axis=1, keepdims=True)

    hi = jnp.full((N_EXPERTS, 1, 1), 2.0, _f32)
    for shift in (64, 32, 16, 8, 4, 2, 1):
        cand = hi * (2.0 ** -shift)
        hi = jnp.where(count_ge(cand) < cap, cand, hi)
    lo = jnp.where(count_ge(hi * 0.5) >= cap, hi * 0.5, 0.0)

    def bisect(i, bracket):
        lo, hi = bracket
        mid = 0.5 * (lo + hi)
        keep = count_ge(mid) >= cap
        return jnp.where(keep, mid, lo), jnp.where(keep, hi, mid)

    lo, hi = lax.fori_loop(0, ROUTE_BISECTIONS, bisect, (lo, hi))
    above = jnp.where(aff >= hi, 1.0, 0.0)
    gt_scr[...] = above
    eq_scr[...] = jnp.where(aff >= lo, 1.0, 0.0) - above
    need_scr[...] = jnp.broadcast_to(cap - count_ge(hi), need_scr.shape)

    r_io = lax.broadcasted_iota(jnp.int32, (nb, LANES), 0)
    c_io = lax.broadcasted_iota(jnp.int32, (nb, LANES), 1)
    triu = jnp.where(r_io <= c_io, 1.0, 0.0).astype(_bf16)
    tril = jnp.where(c_io <= r_io, 1.0, 0.0).astype(_bf16)
    reps = cap // LANES
    slot = lax.broadcasted_iota(jnp.int32, (nb, cap), 1).astype(_f32)
    row = lax.broadcasted_iota(jnp.int32, (nb, cap), 0).astype(_f32)

    def prefix(m):
        loc = _dot(m.astype(_bf16), triu)
        tot = jnp.broadcast_to(loc[:, LANES - 1:LANES], loc.shape)
        return loc, tot, _dot(tril, tot.astype(_bf16))

    def per_expert(e, carry):
        gte = gt_scr[e]
        eqe = eq_scr[e]
        need = need_scr[e][0:1, :]
        eloc, etot, ebinc = prefix(eqe)
        eq_before = ebinc - etot + eloc - eqe
        sel = jnp.maximum(gte, eqe * jnp.where(eq_before < need, 1.0, 0.0))
        loc, tot, binc = prefix(sel)
        bexc = binc - tot
        pos = bexc + loc - sel
        selpos_ref[e] = jnp.where(sel > 0.0, pos, -1.0).astype(jnp.int32)
        bex_ref[e] = bexc.T[0:8, :]

        binc_t = jnp.tile(binc, (1, reps))
        bexc_t = jnp.tile(bexc, (1, reps))
        blk = jnp.sum(jnp.where(binc_t <= slot, 1.0, 0.0), axis=0, keepdims=True)
        hit = row == blk
        onehot = jnp.where(hit, 1.0, 0.0).astype(_bf16)
        rank = slot[0:1, :] - jnp.sum(jnp.where(hit, bexc_t, 0.0), axis=0, keepdims=True)
        loc_rows = _dot(loc.T.astype(_bf16), onehot)
        within = jnp.sum(jnp.where(loc_rows <= rank, 1.0, 0.0), axis=0, keepdims=True)
        token = jnp.minimum(blk * LANES + within, n_tok - 1.0)
        idx_ref[e] = jnp.broadcast_to(token, (8, cap)).astype(jnp.int32)

        at = aff_ref[e].T
        a0 = at.astype(_bf16)
        r1 = at - a0.astype(_f32)
        a1 = r1.astype(_bf16)
        a2 = (r1 - a1.astype(_f32)).astype(_bf16)
        aff_rows = _dot(a0, onehot) + _dot(a1, onehot) + _dot(a2, onehot)
        gate = jnp.sum(jnp.where(row == within, aff_rows, 0.0), axis=0, keepdims=True)
        gate_ref[e] = jnp.broadcast_to(gate, (8, cap))
        return carry

    lax.fori_loop(0, N_EXPERTS, per_expert, 0)


def _route(aff3, cap, n_tok):
    nb = ROUTE_BLOCKS
    full = lambda shape: pl.BlockSpec(shape, lambda: (0,) * len(shape))
    return pl.pallas_call(
        functools.partial(_route_kernel, cap=cap, n_tok=n_tok),
        in_specs=[full((N_EXPERTS, nb, LANES))],
        out_specs=[full((N_EXPERTS, 8, cap)), full((N_EXPERTS, 8, cap)), full((N_EXPERTS, nb, LANES)),
                   full((N_EXPERTS, 8, LANES))],
        out_shape=[jax.ShapeDtypeStruct((N_EXPERTS, 8, cap), jnp.int32),
                   jax.ShapeDtypeStruct((N_EXPERTS, 8, cap), _f32),
                   jax.ShapeDtypeStruct((N_EXPERTS, nb, LANES), jnp.int32),
                   jax.ShapeDtypeStruct((N_EXPERTS, 8, LANES), _f32)],
        scratch_shapes=[pltpu.VMEM((N_EXPERTS, nb, LANES), _f32), pltpu.VMEM((N_EXPERTS, nb, LANES), _f32),
                        pltpu.VMEM((N_EXPERTS, 8, LANES), _f32)],
        compiler_params=pltpu.CompilerParams(vmem_limit_bytes=VMEM_LIMIT),
        name="route",
    )(aff3)


def _ffn_kernel(idx0_ref, idxn_ref, gate_ref, h_hbm, wg_ref, wu_ref, wd_ref, o_ref, rows_scr, xe_scr, acc_scr,
                sem, *, tc, n_tiles, nf):
    f = pl.program_id(2)
    g = pl.program_id(0) * pl.num_programs(1) + pl.program_id(1)
    per_step = tc // nf

    def row_copy(idx_ref, r):
        t = idx_ref[0, 0, r]
        return pltpu.make_async_copy(h_hbm.at[pl.ds(t, 1), :], rows_scr.at[pl.ds(r, 1), :], sem)

    @pl.when((g == 0) & (f == 0))
    def _():
        def issue(r, carry):
            row_copy(idx0_ref, r).start()
            return carry

        lax.fori_loop(0, tc, issue, 0)

    @pl.when(f == 0)
    def _():
        pltpu.make_async_copy(h_hbm.at[pl.ds(0, tc), :], rows_scr, sem).wait()
        xe_scr[...] = rows_scr[...].astype(_bf16)
        acc_scr[...] = jnp.zeros_like(acc_scr)

    wg = wg_ref[...].astype(_bf16)
    wu = wu_ref[...].astype(_bf16)
    wd = wd_ref[...].astype(_bf16)
    hc = tc // 2

    def rows(r0):
        xe = xe_scr[r0:r0 + hc, :]
        hg = _dot(xe, wg)
        hu = _dot(xe, wu)
        yield
        a = (hg * (1.0 / (1.0 + jnp.exp(-hg))) * hu).astype(_bf16)
        acc_scr[r0:r0 + hc, :] += _dot(a, wd)

    _run_staggered([rows(0), rows(hc)], newest_first=True)

    @pl.when(g + 1 < n_tiles)
    def _():
        for r in range(per_step):
            row_copy(idxn_ref, f * per_step + r).start()

    @pl.when(f == nf - 1)
    def _():
        gate_col = jnp.broadcast_to(gate_ref[0, 0:1, :], (LANES, tc)).T[:, 0:1]
        o_ref[0] = (acc_scr[...] * gate_col).astype(_bf16)


def _ffn(idx3, gate3, h2, w_gate, w_up, w_down, cap):
    n, d = h2.shape
    tc = min(1024, cap)
    nc = cap // tc
    tf = min(256, D_FF)
    n_tiles = N_EXPERTS * nc
    idx_spec = lambda off: pl.BlockSpec(
        (1, 1, tc), lambda e, i, f: (jnp.minimum(e * nc + i + off, n_tiles - 1), 0, 0), memory_space=pltpu.SMEM)
    return pl.pallas_call(
        functools.partial(_ffn_kernel, tc=tc, n_tiles=n_tiles, nf=D_FF // tf),
        grid=(N_EXPERTS, nc, D_FF // tf),
        in_specs=[idx_spec(0), idx_spec(1),
                  pl.BlockSpec((1, 8, tc), lambda e, i, f: (e, 0, i)),
                  pl.BlockSpec(memory_space=pl.ANY),
                  pl.BlockSpec((None, d, tf), lambda e, i, f: (e, 0, f)),
                  pl.BlockSpec((None, d, tf), lambda e, i, f: (e, 0, f)),
                  pl.BlockSpec((None, tf, d), lambda e, i, f: (e, f, 0))],
        out_specs=pl.BlockSpec((1, tc, d), lambda e, i, f: (e, i, 0)),
        out_shape=jax.ShapeDtypeStruct((N_EXPERTS, cap, d), _bf16),
        scratch_shapes=[pltpu.VMEM((tc, d), _f32), pltpu.VMEM((tc, d), _bf16),
                        pltpu.VMEM((tc, d), _f32), pltpu.SemaphoreType.DMA(())],
        compiler_params=_params(("arbitrary",) * 3),
        name="ffn",
    )(idx3, idx3, gate3, h2, w_gate, w_up, w_down)


_WIN = LANES
_ALIGN = 16
_CHUNK = _WIN - _ALIGN
_GROUP = 4


def _combine_kernel(base_ref, basen_ref, cnt_ref, pos_ref, x1_ref, gt_ref, g_ref, ye_hbm, o_ref, stage, xstage,
                    y_scr, sems, xsem, *, cap):
    tt = pos_ref.shape[0]
    d = stage.shape[3]
    t = pl.program_id(0) * pl.num_programs(1) + pl.program_id(1)
    n_t = pl.num_programs(0) * pl.num_programs(1)
    slot = t & 1
    r_io = lax.broadcasted_iota(jnp.int32, (tt, _WIN), 1)

    def window(lo):
        aligned = lax.shift_left(lax.shift_right_logical(lo, 4), 4)
        return pl.multiple_of(jnp.minimum(aligned, cap - _WIN), _ALIGN)

    def copy(e, ws, dst_slot):
        return pltpu.make_async_copy(ye_hbm.at[e, pl.ds(ws, _WIN), :], stage.at[dst_slot, e], sems.at[dst_slot])

    def onehot(e, ws, lo):
        want = jnp.where((r_io[0:1] >= lo - ws) & (r_io[0:1] < lo - ws + _CHUNK), r_io[0:1] + ws, -2)
        p = jnp.broadcast_to(pos_ref[:, e:e + 1], (tt, _WIN))
        return jnp.where(p == want, 1.0, 0.0).astype(_bf16)

    @pl.when(t == 0)
    def _():
        for e in range(N_EXPERTS):
            copy(e, window(base_ref[0, 0, e]), 0).start()

    @pl.when(t + 1 < n_t)
    def _():
        for e in range(N_EXPERTS):
            copy(e, window(basen_ref[0, 0, e]), 1 - slot).start()

    los = [base_ref[0, 0, e] for e in range(N_EXPERTS)]
    for e, lo in enumerate(los):
        copy(e, window(lo), slot).wait()
    y = None
    for e0 in range(0, N_EXPERTS, _GROUP):
        oh = jnp.concatenate([onehot(e, window(los[e]), los[e]) for e in range(e0, e0 + _GROUP)], axis=1)
        part = _dot(oh, stage[slot, e0:e0 + _GROUP].reshape(_GROUP * _WIN, d))
        y = part if y is None else y + part
    y_scr[...] = y

    for e in range(N_EXPERTS):
        cnt = cnt_ref[0, 0, e]
        n_chunks = sum((cnt > kc * _CHUNK).astype(jnp.int32) for kc in range(-(-tt // _CHUNK)))

        def extra(kc, carry, e=e):
            lo = los[e] + kc * _CHUNK
            ws = window(lo)
            cp = pltpu.make_async_copy(ye_hbm.at[e, pl.ds(ws, _WIN), :], xstage, xsem)
            cp.start()
            cp.wait()
            y_scr[...] += _dot(onehot(e, ws, lo), xstage[...])
            return carry

        lax.fori_loop(1, n_chunks, extra, 0)

    o_ref[0] = x1_ref[0] + gt_ref[0] * _rms(y_scr[...], g_ref[...])


def _combine(base, cnt, pos, x1, gt, g, ye, cap):
    b, s, d = x1.shape
    tt = min(512, s)
    nt = s // tt
    n_t = b * nt
    mod = pl.BlockSpec((1, 1, d), lambda bi, i: (bi, 0, 0))
    smem = lambda off: pl.BlockSpec(
        (1, 1, N_EXPERTS), lambda bi, i: (jnp.minimum(bi * nt + i + off, n_t - 1), 0, 0), memory_space=pltpu.SMEM)
    return pl.pallas_call(
        functools.partial(_combine_kernel, cap=cap),
        grid=(b, nt),
        in_specs=[smem(0), smem(1), smem(0),
                  pl.BlockSpec((tt, N_EXPERTS), lambda bi, i: (bi * nt + i, 0)),
                  pl.BlockSpec((1, tt, d), lambda bi, i: (bi, i, 0)), mod, _const_spec(g.shape),
                  pl.BlockSpec(memory_space=pl.ANY)],
        out_specs=pl.BlockSpec((1, tt, d), lambda bi, i: (bi, i, 0)),
        out_shape=jax.ShapeDtypeStruct((b, s, d), _f32),
        scratch_shapes=[pltpu.VMEM((2, N_EXPERTS, _WIN, d), _bf16), pltpu.VMEM((_WIN, d), _bf16),
                        pltpu.VMEM((tt, d), _f32), pltpu.SemaphoreType.DMA((2,)), pltpu.SemaphoreType.DMA(())],
        compiler_params=_params(("arbitrary", "arbitrary")),
        name="combine",
    )(base, base, cnt, pos, x1, gt, g, ye)


def _prep_weights(w_in, w_uq, w_ukv, w_out, w_router):
    w1 = w_in.astype(_bf16)
    wq = w_uq.astype(_bf16).reshape(Q_LORA, HEADS, MLA_QK)
    wqn = wq[:, :, :HEAD_W].reshape(Q_LORA, GROUP_W)
    wqr = jnp.pad(wq[:, :, HEAD_W:], ((0, 0), (0, 0), (0, LANES - MLA_ROPE))).reshape(Q_LORA, HEADS * LANES)
    wkv = w_ukv.astype(_bf16).reshape(KV_LORA, HEADS, 2, HEAD_W).transpose(0, 2, 1, 3)
    wkv = wkv.reshape(KV_LORA, 2 * GROUP_W)
    wo = w_out.astype(_bf16)
    wrh, wrl = _split(w_router)
    wr2 = jnp.pad(jnp.concatenate([wrh, wrl], axis=1), ((0, 0), (0, LANES - 2 * N_EXPERTS)))
    return w1, wqn, wqr, wkv, wo, wr2


def _rope_tables(s):
    half = DIFF_QK // 2
    inv = ROPE_THETA ** (-jnp.arange(half, dtype=_f32) / half)
    ang = jnp.arange(s, dtype=_f32)[:, None] * inv[None, :]
    cos = jnp.cos(ang)
    sin = jnp.sin(ang)
    return jnp.tile(cos, (1, 4)), jnp.concatenate([-sin, sin, -sin, sin], axis=1)


def _layer(x, mod, wts, lam_init):
    (g_pre_attn, g_post_attn, g_pre_ffn, g_post_ffn, lq1, lk1, lq2, lk2, g_diff_sub, g_q_lat, g_kv_lat,
     g_mla_out, w1, wqn, wqr, wkv, wo, wr, w_gate, w_up, w_down) = wts
    b, s, d = x.shape
    n = b * s
    cap = CAPACITY_FACTOR * n // N_EXPERTS
    sh_a, sc_a, gt_a, sh_f, sc_f, gt_f = [mod[:, None, i * d:(i + 1) * d] for i in range(6)]
    cos, sin = _rope_tables(s)

    qd, kd, vdt, qn, qr, kn, vmt, kpe = _proj(x, sc_a, sh_a, g_pre_attn, w1, g_q_lat, g_kv_lat, wqn, wqr, wkv,
                                              cos, sin)
    oa = _dattn(qd, kd, vdt, lq1, lk1, lq2, lk2, g_diff_sub, lam_init)
    ob = _mattn(qn, qr, kn, vmt, kpe)
    x1, h2, aff_t = _post(x, oa, ob, gt_a, sc_f, sh_f, g_mla_out, g_post_attn, g_pre_ffn, wo, wr)

    n_pad = ROUTE_BLOCKS * LANES
    aff3 = jnp.pad(aff_t, ((0, 0), (0, n_pad - n))).reshape(N_EXPERTS, ROUTE_BLOCKS, LANES)
    idx8, gate8, selpos, bex8 = _route(aff3, cap, n)

    tc = min(1024, cap)
    idx3 = idx8[:, 0, :].reshape(N_EXPERTS * (cap // tc), 1, tc)
    ye = _ffn(idx3, gate8, h2.reshape(n, d), w_gate, w_up, w_down, cap)

    tt = min(512, s)
    n_tiles = n // tt
    bex = bex8[:, 0, :].astype(jnp.int32).T
    base = bex[::tt // LANES][:n_tiles]
    cnt = jnp.concatenate([base[1:], jnp.full((1, N_EXPERTS), cap, jnp.int32)], axis=0) - base
    pos = selpos.reshape(N_EXPERTS, n_pad).T[:n]
    return _combine(base.reshape(n_tiles, 1, N_EXPERTS), cnt.reshape(n_tiles, 1, N_EXPERTS), pos, x1, gt_f,
                    g_post_ffn, ye, cap)


def kernel(x_prompt, x_sample, c_prompt, c_sample, w_ada, b_ada, g_pre_attn, g_post_attn, g_pre_ffn, g_post_ffn, w_in, lam_q1, lam_k1, lam_q2, lam_k2, g_diff_sub, g_q_lat, w_uq, g_kv_lat, w_ukv, g_mla_out, w_out, w_router, w_gate, w_up, w_down):
    y_prompt, y_sample = x_prompt, x_sample
    bp = c_prompt.shape[0]
    bs = c_sample.shape[0]
    rows = -(-(bp + bs) // 8) * 8
    c_all = jnp.concatenate([c_prompt, c_sample, jnp.zeros((rows - bp - bs, c_prompt.shape[1]), _f32)], axis=0)
    for l in range(DEPTH):
        lam_init = 0.8 - 0.6 * math.exp(-0.3 * l)
        mod = _ada(c_all, w_ada[l], b_ada[l][None, :])
        row = lambda a: a[l][None, :]
        wts = (row(g_pre_attn), row(g_post_attn), row(g_pre_ffn), row(g_post_ffn), row(lam_q1), row(lam_k1),
               row(lam_q2), row(lam_k2), row(g_diff_sub), row(g_q_lat), row(g_kv_lat), row(g_mla_out),
               *_prep_weights(w_in[l], w_uq[l], w_ukv[l], w_out[l], w_router[l]),
               w_gate[l], w_up[l], w_down[l])
        y_prompt = _layer(y_prompt, mod[:bp], wts, lam_init)
        y_sample = _layer(y_sample, mod[bp:bp + bs], wts, lam_init)
    return (y_prompt, y_sample)
```

```python
import functools
import math

import jax
import jax.numpy as jnp
from jax import lax
from jax.experimental import pallas as pl
from jax.experimental.pallas import tpu as pltpu

D_MODEL = 2048
DEPTH = 1
ROPE_THETA = 10000.0
NORM_EPS = 1e-6

HEADS = D_MODEL // 256
DIFF_QK = 64
HEAD_W = 128
MLA_ROPE = 64
MLA_QK = HEAD_W + MLA_ROPE
Q_LORA = D_MODEL // 4
KV_LORA = D_MODEL // 8
GROUP_W = HEADS * HEAD_W
N_EXPERTS = 16
CAPACITY_FACTOR = 2
D_FF = D_MODEL

LANES = 128
ROUTE_BLOCKS = 128
ROUTE_BISECTIONS = 40
VMEM_LIMIT = 56 * 1024 * 1024
ATTN_TQ = 256
ATTN_KEY_CHUNK = 1024
POST_SUB = 256
POST_STREAMS = 2
ATTN_HEADS_PER_STEP = 8

_O_QD = 0
_O_KD = _O_QD + GROUP_W
_O_VD = _O_KD + GROUP_W
_O_MQ = _O_VD + GROUP_W
_O_MKV = _O_MQ + Q_LORA
_O_KR = _O_MKV + KV_LORA
_W1_COLS = _O_KR + LANES

_f32 = jnp.float32
_bf16 = jnp.bfloat16
_LOG2E = math.log2(math.e)


def _dot(a, b):
    return jnp.dot(a, b, preferred_element_type=_f32)


def _dot_nt(a, b):
    return lax.dot_general(a, b, (((1,), (1,)), ((), ())), preferred_element_type=_f32)


def _split(x):
    hi = x.astype(_bf16)
    lo = (x - hi.astype(_f32)).astype(_bf16)
    return hi, lo


def _dot3(a, b):
    ah, al = _split(a)
    bh, bl = _split(b)
    return _dot(ah, bh) + _dot(al, bh) + _dot(ah, bl)


def _rms(x, g):
    return x * lax.rsqrt(jnp.mean(x * x, axis=-1, keepdims=True) + NORM_EPS) * g


def _rope(x, cos, sin):
    lane = lax.broadcasted_iota(jnp.int32, x.shape, 1)
    partner = jnp.where((lane & 63) < 32, pltpu.roll(x, LANES - 32, 1), pltpu.roll(x, 32, 1))
    return x * cos + partner * sin


def _params(sem, vmem=VMEM_LIMIT):
    return pltpu.CompilerParams(dimension_semantics=sem, vmem_limit_bytes=vmem)


def _const_spec(shape):
    nd = len(shape)
    return pl.BlockSpec(shape, lambda *_: (0,) * nd, pipeline_mode=pl.Buffered(1))


def _ada_kernel(c_ref, w_ref, b_ref, o_ref):
    c = c_ref[...]
    s = c * (1.0 / (1.0 + jnp.exp(-c)))
    o_ref[...] = _dot3(s, w_ref[...]) + b_ref[...]


def _ada(c, w, b):
    rows, d = c.shape
    cols = w.shape[1]
    tn = 1024
    return pl.pallas_call(
        _ada_kernel,
        grid=(cols // tn,),
        in_specs=[pl.BlockSpec((rows, d), lambda j: (0, 0)),
                  pl.BlockSpec((d, tn), lambda j: (0, j)),
                  pl.BlockSpec((1, tn), lambda j: (0, j))],
        out_specs=pl.BlockSpec((rows, tn), lambda j: (0, j)),
        out_shape=jax.ShapeDtypeStruct((rows, cols), _f32),
        compiler_params=_params(("arbitrary",)),
        name="ada",
    )(c, w, b)


def _proj_kernel(x_ref, sc_ref, sh_ref, g_ref, w1_ref, gq_ref, gkv_ref, wqn_ref, wqr_ref, wkv_ref,
                 cos_ref, sin_ref, qd_ref, kd_ref, vdt_ref, qn_ref, qr_ref, kn_ref, vmt_ref, kpe_ref):
    x = x_ref[0]
    h = _rms(x, g_ref[...]) * (1.0 + sc_ref[0]) + sh_ref[0]
    hb = h.astype(_bf16)
    cos = cos_ref[...]
    sin = sin_ref[...]

    ql = _rms(_dot(hb, w1_ref[:, _O_MQ:_O_MQ + Q_LORA]), gq_ref[...]).astype(_bf16)
    kvl = _rms(_dot(hb, w1_ref[:, _O_MKV:_O_MKV + KV_LORA]), gkv_ref[...]).astype(_bf16)
    kr = _dot(hb, w1_ref[:, _O_KR:_O_KR + MLA_ROPE])
    kr = jnp.concatenate([kr, jnp.zeros_like(kr)], axis=1)
    kpe_ref[0] = _rope(kr, cos, sin).astype(_bf16)

    q = _dot(hb, w1_ref[:, _O_QD:_O_QD + GROUP_W])
    for hd in range(HEADS):
        sl = slice(hd * LANES, (hd + 1) * LANES)
        qd_ref[0, :, sl] = (_rope(q[:, sl], cos, sin) * (DIFF_QK ** -0.5 * _LOG2E)).astype(_bf16)
    k = _dot(hb, w1_ref[:, _O_KD:_O_KD + GROUP_W])
    for hd in range(HEADS):
        sl = slice(hd * LANES, (hd + 1) * LANES)
        kd_ref[0, :, sl] = _rope(k[:, sl], cos, sin).astype(_bf16)
    vdt_ref[0] = _dot(hb, w1_ref[:, _O_VD:_O_VD + GROUP_W]).T.astype(_bf16)

    mscale = MLA_QK ** -0.5 * _LOG2E
    qn_ref[0] = (_dot(ql, wqn_ref[...]) * mscale).astype(_bf16)
    qr = _dot(ql, wqr_ref[...])
    for hd in range(HEADS):
        sl = slice(hd * LANES, (hd + 1) * LANES)
        qr_ref[0, :, sl] = (_rope(qr[:, sl], cos, sin) * mscale).astype(_bf16)
    kv = _dot(kvl, wkv_ref[...])
    kn_ref[0] = kv[:, :GROUP_W].astype(_bf16)
    vmt_ref[0] = kv[:, GROUP_W:].T.astype(_bf16)


def _proj(x, sc, sh, g, w1, gq, gkv, wqn, wqr, wkv, cos, sin):
    b, s, d = x.shape
    tm = min(512, s)
    row = lambda width: pl.BlockSpec((1, tm, width), lambda bi, i: (bi, i, 0))
    mod = pl.BlockSpec((1, 1, d), lambda bi, i: (bi, 0, 0))
    tab = pl.BlockSpec((tm, LANES), lambda bi, i: (i, 0))
    col = pl.BlockSpec((1, GROUP_W, tm), lambda bi, i: (bi, 0, i))
    rows = jax.ShapeDtypeStruct((b, s, GROUP_W), _bf16)
    cols = jax.ShapeDtypeStruct((b, GROUP_W, s), _bf16)
    return pl.pallas_call(
        _proj_kernel,
        grid=(b, s // tm),
        in_specs=[row(d), mod, mod, _const_spec(g.shape), _const_spec(w1.shape), _const_spec(gq.shape),
                  _const_spec(gkv.shape), _const_spec(wqn.shape), _const_spec(wqr.shape),
                  _const_spec(wkv.shape), tab, tab],
        out_specs=[row(GROUP_W), row(GROUP_W), col, row(GROUP_W), row(GROUP_W), row(GROUP_W), col, row(LANES)],
        out_shape=[rows, rows, cols, rows, rows, rows, cols, jax.ShapeDtypeStruct((b, s, LANES), _bf16)],
        compiler_params=_params(("arbitrary", "arbitrary")),
        name="proj",
    )(x, sc, sh, g, w1, gq, gkv, wqn, wqr, wkv, cos, sin)


def _attend_t(k, q, vt, finish):
    s = k.shape[0]
    ck = min(ATTN_KEY_CHUNK, s)
    n = s // ck
    scores = lambda c: _dot_nt(k[c * ck:(c + 1) * ck], q)

    def fold(o, m_o, pend):
        c, p, m_c = pend
        oc = _dot(vt[:, c * ck:(c + 1) * ck], p)
        return (oc if o is None else o * jnp.exp2(m_o - m_c) + oc), m_c

    m = l = o = m_o = pend = None
    st_next = scores(0)
    yield
    for c in range(n):
        st = st_next
        if c + 1 < n:
            st_next = scores(c + 1)
        if pend is not None:
            o, m_o = fold(o, m_o, pend)
        mc = jnp.max(st, axis=0, keepdims=True)
        m_new = mc if c == 0 else jnp.maximum(m, mc)
        p = jnp.exp2(st - m_new)
        lc = jnp.sum(p, axis=0, keepdims=True)
        l = lc if c == 0 else l * jnp.exp2(m - m_new) + lc
        m = m_new
        pend = (c, p.astype(_bf16), m_new)
        yield
    o, m_o = fold(o, m_o, pend)
    finish(o * (1.0 / l))


def _run_staggered(streams, newest_first=False):
    live = []
    pending = list(streams)
    while live or pending:
        if pending:
            live.append(pending.pop(0))
        for g in (list(reversed(live)) if newest_first else list(live)):
            try:
                next(g)
            except StopIteration:
                live.remove(g)


def _dattn_kernel(lq1_ref, lk1_ref, lq2_ref, lk2_ref, g_ref, q_ref, k_ref, vt_ref, o_ref, *, lam_init):
    tq = q_ref.shape[1]
    lam = (jnp.exp(jnp.sum(lq1_ref[...] * lk1_ref[...], axis=-1, keepdims=True))
           - jnp.exp(jnp.sum(lq2_ref[...] * lk2_ref[...], axis=-1, keepdims=True)) + lam_init)
    lane = lax.broadcasted_iota(jnp.int32, (tq, LANES), 1)
    first = lane < DIFF_QK

    def head(hd):
        sl = slice(hd * LANES, (hd + 1) * LANES)
        q = q_ref[0, :, sl]
        zero = jnp.zeros_like(q)

        def finish(o12):
            o = (o12[:, :tq] - lam * o12[:, tq:]).T
            o_ref[0, :, sl] = (_rms(o, g_ref[...]) * (1.0 - lam_init)).astype(_bf16)

        q12 = jnp.concatenate([jnp.where(first, q, zero), jnp.where(first, zero, q)], axis=0)
        return _attend_t(k_ref[0, :, sl], q12, vt_ref[0, sl, :], finish)

    _run_staggered([head(hd) for hd in range(ATTN_HEADS_PER_STEP)], newest_first=True)


def _dattn(qd, kd, vdt, lq1, lk1, lq2, lk2, g, lam_init):
    b, s, _ = qd.shape
    tq = min(ATTN_TQ, s)
    w = ATTN_HEADS_PER_STEP * LANES
    small = lambda a: pl.BlockSpec(a.shape, lambda bi, h, i: (0, 0))
    return pl.pallas_call(
        functools.partial(_dattn_kernel, lam_init=lam_init),
        grid=(b, HEADS // ATTN_HEADS_PER_STEP, s // tq),
        in_specs=[small(lq1), small(lk1), small(lq2), small(lk2), small(g),
                  pl.BlockSpec((1, tq, w), lambda bi, h, i: (bi, i, h)),
                  pl.BlockSpec((1, s, w), lambda bi, h, i: (bi, 0, h)),
                  pl.BlockSpec((1, w, s), lambda bi, h, i: (bi, h, 0))],
        out_specs=pl.BlockSpec((1, tq, w), lambda bi, h, i: (bi, i, h)),
        out_shape=jax.ShapeDtypeStruct((b, s, GROUP_W), _bf16),
        compiler_params=_params(("arbitrary",) * 3),
        name="dattn",
    )(lq1, lk1, lq2, lk2, g, qd, kd, vdt)


def _mattn_kernel(qn_ref, qr_ref, kn_ref, vt_ref, kpe_ref, o_ref):
    kpe = kpe_ref[0]

    def head(hd):
        sl = slice(hd * LANES, (hd + 1) * LANES)

        def finish(o):
            o_ref[0, :, sl] = o.T.astype(_bf16)

        q = jnp.concatenate([qn_ref[0, :, sl], qr_ref[0, :, sl]], axis=-1)
        k = jnp.concatenate([kn_ref[0, :, sl], kpe], axis=-1)
        return _attend_t(k, q, vt_ref[0, sl, :], finish)

    _run_staggered([head(hd) for hd in range(ATTN_HEADS_PER_STEP)], newest_first=True)


def _mattn(qn, qr, kn, vmt, kpe):
    b, s, _ = qn.shape
    tq = min(2 * ATTN_TQ, s)
    w = ATTN_HEADS_PER_STEP * LANES
    return pl.pallas_call(
        _mattn_kernel,
        grid=(b, HEADS // ATTN_HEADS_PER_STEP, s // tq),
        in_specs=[pl.BlockSpec((1, tq, w), lambda bi, h, i: (bi, i, h)),
                  pl.BlockSpec((1, tq, w), lambda bi, h, i: (bi, i, h)),
                  pl.BlockSpec((1, s, w), lambda bi, h, i: (bi, 0, h)),
                  pl.BlockSpec((1, w, s), lambda bi, h, i: (bi, h, 0)),
                  pl.BlockSpec((1, s, LANES), lambda bi, h, i: (bi, 0, 0))],
        out_specs=pl.BlockSpec((1, tq, w), lambda bi, h, i: (bi, i, h)),
        out_shape=jax.ShapeDtypeStruct((b, s, GROUP_W), _bf16),
        compiler_params=_params(("arbitrary",) * 3),
        name="mattn",
    )(qn, qr, kn, vmt, kpe)


def _post_kernel(x_ref, oa_ref, ob_ref, gt_ref, sc_ref, sh_ref, gmla_ref, gpost_ref, gpre_ref,
                 wo_ref, wr_ref, x1_ref, h2_ref, aff_ref):
    tm = x_ref.shape[1]

    def rows(r0):
        sl = slice(r0, r0 + POST_SUB)
        obn = _rms(ob_ref[0, sl, :].astype(_f32), gmla_ref[...]).astype(_bf16)
        m = _dot(oa_ref[0, sl, :], wo_ref[:GROUP_W, :]) + _dot(obn, wo_ref[GROUP_W:, :])
        yield
        x1 = x_ref[0, sl, :] + gt_ref[0] * _rms(m, gpost_ref[...])
        x1_ref[0, sl, :] = x1
        h2 = _rms(x1, gpre_ref[...]) * (1.0 + sc_ref[0]) + sh_ref[0]
        h2_ref[0, sl, :] = h2
        hh, hl = _split(h2)
        parts = _dot(hh, wr_ref[...]) + _dot(hl, wr_ref[...])
        yield
        logits = parts + pltpu.roll(parts, LANES - N_EXPERTS, 1)
        lane = lax.broadcasted_iota(jnp.int32, logits.shape, 1)
        logits = jnp.where(lane < N_EXPERTS, logits, -jnp.inf)
        p = jnp.exp(logits - jnp.max(logits, axis=-1, keepdims=True))
        aff = p / jnp.sum(p, axis=-1, keepdims=True)
        aff_ref[:, sl] = aff.T[0:N_EXPERTS, :]

    _run_staggered([rows(r0) for r0 in range(0, tm, POST_SUB)], newest_first=True)


def _post(x, oa, ob, gt, sc, sh, gmla, gpost, gpre, wo, wr):
    b, s, d = x.shape
    tm = min(POST_STREAMS * POST_SUB, s)
    nt = s // tm
    row = lambda width: pl.BlockSpec((1, tm, width), lambda bi, i: (bi, i, 0))
    mod = pl.BlockSpec((1, 1, d), lambda bi, i: (bi, 0, 0))
    return pl.pallas_call(
        _post_kernel,
        grid=(b, nt),
        in_specs=[row(d), row(GROUP_W), row(GROUP_W), mod, mod, mod, _const_spec(gmla.shape),
                  _const_spec(gpost.shape), _const_spec(gpre.shape), _const_spec(wo.shape),
                  _const_spec(wr.shape)],
        out_specs=[row(d), row(d), pl.BlockSpec((N_EXPERTS, tm), lambda bi, i: (0, bi * nt + i))],
        out_shape=[jax.ShapeDtypeStruct((b, s, d), _f32), jax.ShapeDtypeStruct((b, s, d), _f32),
                   jax.ShapeDtypeStruct((N_EXPERTS, b * s), _f32)],
        compiler_params=_params(("arbitrary", "arbitrary")),
        name="post",
    )(x, oa, ob, gt, sc, sh, gmla, gpost, gpre, wo, wr)


def _route_kernel(aff_ref, idx_ref, gate_ref, selpos_ref, bex_ref, gt_scr, eq_scr, need_scr, *, cap, n_tok):
    nb = ROUTE_BLOCKS
    aff = aff_ref[...]

    def count_ge(t):
        c = jnp.sum(jnp.where(aff >= t, 1.0, 0.0), axis=2, keepdims=True)
        return jnp.sum(c, axis=1, keepdims=True)

    hi = jnp.full((N_EXPERTS, 1, 1), 2.0, _f32)
    for shift in (64, 32, 16, 8, 4, 2, 1):
        cand = hi * (2.0 ** -shift)
        hi = jnp.where(count_ge(cand) < cap, cand, hi)
    lo = jnp.where(count_ge(hi * 0.5) >= cap, hi * 0.5, 0.0)

    def bisect(i, bracket):
        lo, hi = bracket
        mid = 0.5 * (lo + hi)
        keep = count_ge(mid) >= cap
        return jnp.where(keep, mid, lo), jnp.where(keep, hi, mid)

    lo, hi = lax.fori_loop(0, ROUTE_BISECTIONS, bisect, (lo, hi))
    above = jnp.where(aff >= hi, 1.0, 0.0)
    gt_scr[...] = above
    eq_scr[...] = jnp.where(aff >= lo, 1.0, 0.0) - above
    need_scr[...] = jnp.broadcast_to(cap - count_ge(hi), need_scr.shape)

    r_io = lax.broadcasted_iota(jnp.int32, (nb, LANES), 0)
    c_io = lax.broadcasted_iota(jnp.int32, (nb, LANES), 1)
    triu = jnp.where(r_io <= c_io, 1.0, 0.0).astype(_bf16)
    tril = jnp.where(c_io <= r_io, 1.0, 0.0).astype(_bf16)
    reps = cap // LANES
    slot = lax.broadcasted_iota(jnp.int32, (nb, cap), 1).astype(_f32)
    row = lax.broadcasted_iota(jnp.int32, (nb, cap), 0).astype(_f32)

    def prefix(m):
        loc = _dot(m.astype(_bf16), triu)
        tot = jnp.broadcast_to(loc[:, LANES - 1:LANES], loc.shape)
        return loc, tot, _dot(tril, tot.astype(_bf16))

    def per_expert(e, carry):
        gte = gt_scr[e]
        eqe = eq_scr[e]
        need = need_scr[e][0:1, :]
        eloc, etot, ebinc = prefix(eqe)
        eq_before = ebinc - etot + eloc - eqe
        sel = jnp.maximum(gte, eqe * jnp.where(eq_before < need, 1.0, 0.0))
        loc, tot, binc = prefix(sel)
        bexc = binc - tot
        pos = bexc + loc - sel
        selpos_ref[e] = jnp.where(sel > 0.0, pos, -1.0).astype(jnp.int32)
        bex_ref[e] = bexc.T[0:8, :]

        binc_t = jnp.tile(binc, (1, reps))
        bexc_t = jnp.tile(bexc, (1, reps))
        blk = jnp.sum(jnp.where(binc_t <= slot, 1.0, 0.0), axis=0, keepdims=True)
        hit = row == blk
        onehot = jnp.where(hit, 1.0, 0.0).astype(_bf16)
        rank = slot[0:1, :] - jnp.sum(jnp.where(hit, bexc_t, 0.0), axis=0, keepdims=True)
        loc_rows = _dot(loc.T.astype(_bf16), onehot)
        within = jnp.sum(jnp.where(loc_rows <= rank, 1.0, 0.0), axis=0, keepdims=True)
        token = jnp.minimum(blk * LANES + within, n_tok - 1.0)
        idx_ref[e] = jnp.broadcast_to(token, (8, cap)).astype(jnp.int32)

        at = aff_ref[e].T
        a0 = at.astype(_bf16)
        r1 = at - a0.astype(_f32)
        a1 = r1.astype(_bf16)
        a2 = (r1 - a1.astype(_f32)).astype(_bf16)
        aff_rows = _dot(a0, onehot) + _dot(a1, onehot) + _dot(a2, onehot)
        gate = jnp.sum(jnp.where(row == within, aff_rows, 0.0), axis=0, keepdims=True)
        gate_ref[e] = jnp.broadcast_to(gate, (8, cap))
        return carry

    lax.fori_loop(0, N_EXPERTS, per_expert, 0)


def _route(aff3, cap, n_tok):
    nb = ROUTE_BLOCKS
    full = lambda shape: pl.BlockSpec(shape, lambda: (0,) * len(shape))
    return pl.pallas_call(
        functools.partial(_route_kernel, cap=cap, n_tok=n_tok),
        in_specs=[full((N_EXPERTS, nb, LANES))],
        out_specs=[full((N_EXPERTS, 8, cap)), full((N_EXPERTS, 8, cap)), full((N_EXPERTS, nb, LANES)),
                   full((N_EXPERTS, 8, LANES))],
        out_shape=[jax.ShapeDtypeStruct((N_EXPERTS, 8, cap), jnp.int32),
                   jax.ShapeDtypeStruct((N_EXPERTS, 8, cap), _f32),
                   jax.ShapeDtypeStruct((N_EXPERTS, nb, LANES), jnp.int32),
                   jax.ShapeDtypeStruct((N_EXPERTS, 8, LANES), _f32)],
        scratch_shapes=[pltpu.VMEM((N_EXPERTS, nb, LANES), _f32), pltpu.VMEM((N_EXPERTS, nb, LANES), _f32),
                        pltpu.VMEM((N_EXPERTS, 8, LANES), _f32)],
        compiler_params=pltpu.CompilerParams(vmem_limit_bytes=VMEM_LIMIT),
        name="route",
    )(aff3)


def _ffn_kernel(idx0_ref, idxn_ref, gate_ref, h_hbm, wg_ref, wu_ref, wd_ref, o_ref, rows_scr, xe_scr, acc_scr,
                sem, *, tc, n_tiles, nf):
    f = pl.program_id(2)
    g = pl.program_id(0) * pl.num_programs(1) + pl.program_id(1)
    per_step = tc // nf

    def row_copy(idx_ref, r):
        t = idx_ref[0, 0, r]
        return pltpu.make_async_copy(h_hbm.at[pl.ds(t, 1), :], rows_scr.at[pl.ds(r, 1), :], sem)

    @pl.when((g == 0) & (f == 0))
    def _():
        def issue(r, carry):
            row_copy(idx0_ref, r).start()
            return carry

        lax.fori_loop(0, tc, issue, 0)

    @pl.when(f == 0)
    def _():
        pltpu.make_async_copy(h_hbm.at[pl.ds(0, tc), :], rows_scr, sem).wait()
        xe_scr[...] = rows_scr[...].astype(_bf16)
        acc_scr[...] = jnp.zeros_like(acc_scr)

    wg = wg_ref[...].astype(_bf16)
    wu = wu_ref[...].astype(_bf16)
    wd = wd_ref[...].astype(_bf16)
    hc = tc // 2

    def rows(r0):
        xe = xe_scr[r0:r0 + hc, :]
        hg = _dot(xe, wg)
        hu = _dot(xe, wu)
        yield
        a = (hg * (1.0 / (1.0 + jnp.exp(-hg))) * hu).astype(_bf16)
        acc_scr[r0:r0 + hc, :] += _dot(a, wd)

    _run_staggered([rows(0), rows(hc)], newest_first=True)

    @pl.when(g + 1 < n_tiles)
    def _():
        for r in range(per_step):
            row_copy(idxn_ref, f * per_step + r).start()

    @pl.when(f == nf - 1)
    def _():
        gate_col = jnp.broadcast_to(gate_ref[0, 0:1, :], (LANES, tc)).T[:, 0:1]
        o_ref[0] = (acc_scr[...] * gate_col).astype(_bf16)


def _ffn(idx3, gate3, h2, w_gate, w_up, w_down, cap):
    n, d = h2.shape
    tc = min(1024, cap)
    nc = cap // tc
    tf = min(256, D_FF)
    n_tiles = N_EXPERTS * nc
    idx_spec = lambda off: pl.BlockSpec(
        (1, 1, tc), lambda e, i, f: (jnp.minimum(e * nc + i + off, n_tiles - 1), 0, 0), memory_space=pltpu.SMEM)
    return pl.pallas_call(
        functools.partial(_ffn_kernel, tc=tc, n_tiles=n_tiles, nf=D_FF // tf),
        grid=(N_EXPERTS, nc, D_FF // tf),
        in_specs=[idx_spec(0), idx_spec(1),
                  pl.BlockSpec((1, 8, tc), lambda e, i, f: (e, 0, i)),
                  pl.BlockSpec(memory_space=pl.ANY),
                  pl.BlockSpec((None, d, tf), lambda e, i, f: (e, 0, f)),
                  pl.BlockSpec((None, d, tf), lambda e, i, f: (e, 0, f)),
                  pl.BlockSpec((None, tf, d), lambda e, i, f: (e, f, 0))],
        out_specs=pl.BlockSpec((1, tc, d), lambda e, i, f: (e, i, 0)),
        out_shape=jax.ShapeDtypeStruct((N_EXPERTS, cap, d), _bf16),
        scratch_shapes=[pltpu.VMEM((tc, d), _f32), pltpu.VMEM((tc, d), _bf16),
                        pltpu.VMEM((tc, d), _f32), pltpu.SemaphoreType.DMA(())],
        compiler_params=_params(("arbitrary",) * 3),
        name="ffn",
    )(idx3, idx3, gate3, h2, w_gate, w_up, w_down)


_WIN = LANES
_ALIGN = 16
_CHUNK = _WIN - _ALIGN
_GROUP = 4


def _combine_kernel(base_ref, basen_ref, cnt_ref, pos_ref, x1_ref, gt_ref, g_ref, ye_hbm, o_ref, stage, xstage,
                    y_scr, sems, xsem, *, cap):
    tt = pos_ref.shape[0]
    d = stage.shape[3]
    t = pl.program_id(0) * pl.num_programs(1) + pl.program_id(1)
    n_t = pl.num_programs(0) * pl.num_programs(1)
    slot = t & 1
    r_io = lax.broadcasted_iota(jnp.int32, (tt, _WIN), 1)

    def window(lo):
        aligned = lax.shift_left(lax.shift_right_logical(lo, 4), 4)
        return pl.multiple_of(jnp.minimum(aligned, cap - _WIN), _ALIGN)

    def copy(e, ws, dst_slot):
        return pltpu.make_async_copy(ye_hbm.at[e, pl.ds(ws, _WIN), :], stage.at[dst_slot, e], sems.at[dst_slot])

    def onehot(e, ws, lo):
        want = jnp.where((r_io[0:1] >= lo - ws) & (r_io[0:1] < lo - ws + _CHUNK), r_io[0:1] + ws, -2)
        p = jnp.broadcast_to(pos_ref[:, e:e + 1], (tt, _WIN))
        return jnp.where(p == want, 1.0, 0.0).astype(_bf16)

    @pl.when(t == 0)
    def _():
        for e in range(N_EXPERTS):
            copy(e, window(base_ref[0, 0, e]), 0).start()

    @pl.when(t + 1 < n_t)
    def _():
        for e in range(N_EXPERTS):
            copy(e, window(basen_ref[0, 0, e]), 1 - slot).start()

    los = [base_ref[0, 0, e] for e in range(N_EXPERTS)]
    for e, lo in enumerate(los):
        copy(e, window(lo), slot).wait()
    y = None
    for e0 in range(0, N_EXPERTS, _GROUP):
        oh = jnp.concatenate([onehot(e, window(los[e]), los[e]) for e in range(e0, e0 + _GROUP)], axis=1)
        part = _dot(oh, stage[slot, e0:e0 + _GROUP].reshape(_GROUP * _WIN, d))
        y = part if y is None else y + part
    y_scr[...] = y

    for e in range(N_EXPERTS):
        cnt = cnt_ref[0, 0, e]
        n_chunks = sum((cnt > kc * _CHUNK).astype(jnp.int32) for kc in range(-(-tt // _CHUNK)))

        def extra(kc, carry, e=e):
            lo = los[e] + kc * _CHUNK
            ws = window(lo)
            cp = pltpu.make_async_copy(ye_hbm.at[e, pl.ds(ws, _WIN), :], xstage, xsem)
            cp.start()
            cp.wait()
            y_scr[...] += _dot(onehot(e, ws, lo), xstage[...])
            return carry

        lax.fori_loop(1, n_chunks, extra, 0)

    o_ref[0] = x1_ref[0] + gt_ref[0] * _rms(y_scr[...], g_ref[...])


def _combine(base, cnt, pos, x1, gt, g, ye, cap):
    b, s, d = x1.shape
    tt = min(512, s)
    nt = s // tt
    n_t = b * nt
    mod = pl.BlockSpec((1, 1, d), lambda bi, i: (bi, 0, 0))
    smem = lambda off: pl.BlockSpec(
        (1, 1, N_EXPERTS), lambda bi, i: (jnp.minimum(bi * nt + i + off, n_t - 1), 0, 0), memory_space=pltpu.SMEM)
    return pl.pallas_call(
        functools.partial(_combine_kernel, cap=cap),
        grid=(b, nt),
        in_specs=[smem(0), smem(1), smem(0),
                  pl.BlockSpec((tt, N_EXPERTS), lambda bi, i: (bi * nt + i, 0)),
                  pl.BlockSpec((1, tt, d), lambda bi, i: (bi, i, 0)), mod, _const_spec(g.shape),
                  pl.BlockSpec(memory_space=pl.ANY)],
        out_specs=pl.BlockSpec((1, tt, d), lambda bi, i: (bi, i, 0)),
        out_shape=jax.ShapeDtypeStruct((b, s, d), _f32),
        scratch_shapes=[pltpu.VMEM((2, N_EXPERTS, _WIN, d), _bf16), pltpu.VMEM((_WIN, d), _bf16),
                        pltpu.VMEM((tt, d), _f32), pltpu.SemaphoreType.DMA((2,)), pltpu.SemaphoreType.DMA(())],
        compiler_params=_params(("arbitrary", "arbitrary")),
        name="combine",
    )(base, base, cnt, pos, x1, gt, g, ye)


def _prep_weights(w_in, w_uq, w_ukv, w_out, w_router):
    w1 = w_in.astype(_bf16)
    wq = w_uq.astype(_bf16).reshape(Q_LORA, HEADS, MLA_QK)
    wqn = wq[:, :, :HEAD_W].reshape(Q_LORA, GROUP_W)
    wqr = jnp.pad(wq[:, :, HEAD_W:], ((0, 0), (0, 0), (0, LANES - MLA_ROPE))).reshape(Q_LORA, HEADS * LANES)
    wkv = w_ukv.astype(_bf16).reshape(KV_LORA, HEADS, 2, HEAD_W).transpose(0, 2, 1, 3)
    wkv = wkv.reshape(KV_LORA, 2 * GROUP_W)
    wo = w_out.astype(_bf16)
    wrh, wrl = _split(w_router)
    wr2 = jnp.pad(jnp.concatenate([wrh, wrl], axis=1), ((0, 0), (0, LANES - 2 * N_EXPERTS)))
    return w1, wqn, wqr, wkv, wo, wr2


def _rope_tables(s):
    half = DIFF_QK // 2
    inv = ROPE_THETA ** (-jnp.arange(half, dtype=_f32) / half)
    ang = jnp.arange(s, dtype=_f32)[:, None] * inv[None, :]
    cos = jnp.cos(ang)
    sin = jnp.sin(ang)
    return jnp.tile(cos, (1, 4)), jnp.concatenate([-sin, sin, -sin, sin], axis=1)


def _layer(x, mod, wts, lam_init):
    (g_pre_attn, g_post_attn, g_pre_ffn, g_post_ffn, lq1, lk1, lq2, lk2, g_diff_sub, g_q_lat, g_kv_lat,
     g_mla_out, w1, wqn, wqr, wkv, wo, wr, w_gate, w_up, w_down) = wts
    b, s, d = x.shape
    n = b * s
    cap = CAPACITY_FACTOR * n // N_EXPERTS
    sh_a, sc_a, gt_a, sh_f, sc_f, gt_f = [mod[:, None, i * d:(i + 1) * d] for i in range(6)]
    cos, sin = _rope_tables(s)

    qd, kd, vdt, qn, qr, kn, vmt, kpe = _proj(x, sc_a, sh_a, g_pre_attn, w1, g_q_lat, g_kv_lat, wqn, wqr, wkv,
                                              cos, sin)
    oa = _dattn(qd, kd, vdt, lq1, lk1, lq2, lk2, g_diff_sub, lam_init)
    ob = _mattn(qn, qr, kn, vmt, kpe)
    x1, h2, aff_t = _post(x, oa, ob, gt_a, sc_f, sh_f, g_mla_out, g_post_attn, g_pre_ffn, wo, wr)

    n_pad = ROUTE_BLOCKS * LANES
    aff3 = jnp.pad(aff_t, ((0, 0), (0, n_pad - n))).reshape(N_EXPERTS, ROUTE_BLOCKS, LANES)
    idx8, gate8, selpos, bex8 = _route(aff3, cap, n)

    tc = min(1024, cap)
    idx3 = idx8[:, 0, :].reshape(N_EXPERTS * (cap // tc), 1, tc)
    ye = _ffn(idx3, gate8, h2.reshape(n, d), w_gate, w_up, w_down, cap)

    tt = min(512, s)
    n_tiles = n // tt
    bex = bex8[:, 0, :].astype(jnp.int32).T
    base = bex[::tt // LANES][:n_tiles]
    cnt = jnp.concatenate([base[1:], jnp.full((1, N_EXPERTS), cap, jnp.int32)], axis=0) - base
    pos = selpos.reshape(N_EXPERTS, n_pad).T[:n]
    return _combine(base.reshape(n_tiles, 1, N_EXPERTS), cnt.reshape(n_tiles, 1, N_EXPERTS), pos, x1, gt_f,
                    g_post_ffn, ye, cap)


def kernel(x_prompt, x_sample, c_prompt, c_sample, w_ada, b_ada, g_pre_attn, g_post_attn, g_pre_ffn, g_post_ffn, w_in, lam_q1, lam_k1, lam_q2, lam_k2, g_diff_sub, g_q_lat, w_uq, g_kv_lat, w_ukv, g_mla_out, w_out, w_router, w_gate, w_up, w_down):
    y_prompt, y_sample = x_prompt, x_sample
    bp = c_prompt.shape[0]
    bs = c_sample.shape[0]
    rows = -(-(bp + bs) // 8) * 8
    c_all = jnp.concatenate([c_prompt, c_sample, jnp.zeros((rows - bp - bs, c_prompt.shape[1]), _f32)], axis=0)
    for l in range(DEPTH):
        lam_init = 0.8 - 0.6 * math.exp(-0.3 * l)
        mod = _ada(c_all, w_ada[l], b_ada[l][None, :])
        row = lambda a: a[l][None, :]
        wts = (row(g_pre_attn), row(g_post_attn), row(g_pre_ffn), row(g_post_ffn), row(lam_q1), row(lam_k1),
               row(lam_q2), row(lam_k2), row(g_diff_sub), row(g_q_lat), row(g_kv_lat), row(g_mla_out),
               *_prep_weights(w_in[l], w_uq[l], w_ukv[l], w_out[l], w_router[l]),
               w_gate[l], w_up[l], w_down[l])
        y_prompt = _layer(y_prompt, mod[:bp], wts, lam_init)
        y_sample = _layer(y_sample, mod[bp:bp + bs], wts, lam_init)
    return (y_prompt, y_sample)
```
